```python
import jax, jax.numpy as jnp
from jax import lax
import numpy as np

D_MODEL = 1024
BATCH = 4
SEQ = 4096
DEPTH = 2
DEC_BATCH = 128
DEC_SEQ = 4
PAST_LEN = 16384
PAGE_SIZE = 128

N_EVEN = (DEPTH + 1) // 2
N_ODD = DEPTH // 2
A_HEADS = 8
A_KV_HEADS = 2
A_GROUP = A_HEADS // A_KV_HEADS
A_HEAD_DIM = 64
WINDOW = 128
B_HEADS = 4
B_KEY_DIM = 128
B_VAL_DIM = 128
RET_CHUNK = 128
ROPE_BASE = 10000.0
C_HEADS = 16
C_KV_HEADS = 4
C_GROUP = C_HEADS // C_KV_HEADS
C_HEAD_DIM = 64
Q_BLOCK = 128
FORGET_BIAS_INIT = 8.0
N_MEM = 256
MEM_HEADS = 4
MEM_HEAD_DIM = D_MODEL // MEM_HEADS
N_GROUPS = 4
EXPERTS_PER_GROUP = 4
N_EXPERTS = N_GROUPS * EXPERTS_PER_GROUP
EXPERT_TOPK = 2
D_EXPERT = D_MODEL // 2
LN_EPS = 1e-5
GN_EPS = 1e-6
DEEPNORM_ALPHA = (2 * DEPTH) ** 0.25
DEEPNORM_BETA = (8 * DEPTH) ** -0.25
NEG_INF = -1e30

A_Q_W = A_HEADS * A_HEAD_DIM
A_KV_W = A_KV_HEADS * A_HEAD_DIM
B_QK_W = B_HEADS * B_KEY_DIM
B_V_W = B_HEADS * B_VAL_DIM
IN_AB_SPLITS = (A_Q_W, A_KV_W, A_KV_W, B_QK_W, B_QK_W, B_V_W, B_V_W)
IN_AB_W = sum(IN_AB_SPLITS)
MIX_AB_W = A_Q_W + B_V_W
C_Q_W = C_HEADS * C_HEAD_DIM
C_KV_W = C_KV_HEADS * C_HEAD_DIM
IN_FOX_SPLITS = (C_Q_W, C_KV_W, C_KV_W, C_HEADS)
IN_FOX_W = sum(IN_FOX_SPLITS)

F32 = jnp.float32

kernel_name = 'hybrid_swa_retention_fox_hmoe_step'


def _split(z, widths):
    return jnp.split(z, np.cumsum(widths)[:-1].tolist(), axis=-1)


def _layer_norm(x, g, b):
    xf = x.astype(F32)
    mu = jnp.mean(xf, axis=-1, keepdims=True)
    var = jnp.mean(jnp.square(xf - mu), axis=-1, keepdims=True)
    y = (xf - mu) * lax.rsqrt(var + LN_EPS) * g.astype(F32) + b.astype(F32)
    return y.astype(x.dtype)


def _post_norm(x, h, g, b):
    return _layer_norm(DEEPNORM_ALPHA * x + h.astype(x.dtype), g, b)


def _rotary(x, pos):
    half = x.shape[-1] // 2
    inv_freq = 1.0 / (ROPE_BASE ** jnp.linspace(0.0, 1.0, half, dtype=F32))
    ang = pos.astype(F32)[:, None] * inv_freq[None, :]
    cos = jnp.cos(ang)[None, :, None, :]
    sin = jnp.sin(ang)[None, :, None, :]
    xf = x.astype(F32)
    x1, x2 = xf[..., :half], xf[..., half:]
    return jnp.concatenate([x1 * cos - x2 * sin, x2 * cos + x1 * sin], axis=-1)


def _project_ab(x, w_in, pos):
    bsz, t = x.shape[:2]
    aq, ak, av, bq, bk, bv, bg = _split(x @ w_in, IN_AB_SPLITS)
    aq = aq.reshape(bsz, t, A_KV_HEADS, A_GROUP, A_HEAD_DIM)
    ak = ak.reshape(bsz, t, A_KV_HEADS, A_HEAD_DIM)
    av = av.reshape(bsz, t, A_KV_HEADS, A_HEAD_DIM)
    bq = _rotary(bq.reshape(bsz, t, B_HEADS, B_KEY_DIM), pos)
    bk = _rotary(bk.reshape(bsz, t, B_HEADS, B_KEY_DIM), pos) * (B_KEY_DIM ** -0.5)
    bv = bv.reshape(bsz, t, B_HEADS, B_VAL_DIM).astype(F32)
    return aq, ak, av, bq, bk, bv, bg


def _sink_probs(s, mask, sink):
    s = jnp.where(mask, s, NEG_INF)
    sk = sink.astype(F32).reshape(A_KV_HEADS, A_GROUP, 1, 1)
    m = jnp.maximum(jnp.max(s, axis=-1, keepdims=True), sk)
    p = jnp.exp(s - m)
    return p / (jnp.sum(p, axis=-1, keepdims=True) + jnp.exp(sk - m))


def _swa_prompt(q, k, v, sink):
    bsz, t = q.shape[:2]
    nb = t // WINDOW
    qb = q.reshape(bsz, nb, WINDOW, A_KV_HEADS, A_GROUP, A_HEAD_DIM)

    def band(z):
        zb = z.reshape(bsz, nb, WINDOW, A_KV_HEADS, A_HEAD_DIM)
        prev = jnp.pad(zb, ((0, 0), (1, 0), (0, 0), (0, 0), (0, 0)))[:, :-1]
        return jnp.concatenate([prev, zb], axis=2)

    kk, vv = band(k), band(v)
    s = jnp.einsum('bnqkgd,bnskd->bnkgqs', qb, kk, preferred_element_type=F32) * (A_HEAD_DIM ** -0.5)
    qi = jnp.arange(WINDOW)[:, None]
    kj = jnp.arange(2 * WINDOW)[None, :]
    in_band = (kj >= qi) & (kj <= qi + WINDOW)
    blk = jnp.arange(nb)[:, None, None]
    valid = in_band[None] & ((blk > 0) | (kj[None] >= WINDOW))
    p = _sink_probs(s, valid[None, :, None, None], sink)
    o = jnp.einsum('bnkgqs,bnskd->bnqkgd', p, vv.astype(F32))
    return o.reshape(bsz, t, A_Q_W)


def _swa_sample(q, k_new, v_new, k_buf, v_buf, sink):
    bsz, t = q.shape[:2]
    w = k_buf.shape[1]
    kk = jnp.concatenate([k_buf.astype(k_new.dtype), k_new], axis=1)
    vv = jnp.concatenate([v_buf.astype(v_new.dtype), v_new], axis=1)
    s = jnp.einsum('btkgd,bskd->bkgts', q, kk, preferred_element_type=F32) * (A_HEAD_DIM ** -0.5)
    qi = jnp.arange(t)[:, None]
    kj = jnp.arange(w + t)[None, :]
    mask = (kj >= qi + w - WINDOW) & (kj <= qi + w)
    p = _sink_probs(s, mask, sink)
    o = jnp.einsum('bkgts,bskd->btkgd', p, vv.astype(F32))
    return o.reshape(bsz, t, A_Q_W), kk[:, -WINDOW:], vv[:, -WINDOW:]


def _ret_log_gamma():
    return jnp.log(1.0 - 2.0 ** (-5.0 - jnp.arange(B_HEADS, dtype=F32)))


def _retention_chunk(s0, q, k, v):
    c = q.shape[1]
    lg = _ret_log_gamma()
    n = jnp.arange(c, dtype=F32)
    diff = n[:, None] - n[None, :]
    decay = jnp.where(diff >= 0, jnp.exp(lg[:, None, None] * jnp.maximum(diff, 0.0)), 0.0)
    inner = jnp.einsum('bnhd,bmhd->bhnm', q, k) * decay[None]
    o = jnp.einsum('bhnm,bmhv->bnhv', inner, v)
    o = o + jnp.einsum('bnhd,bhdv->bnhv', q, s0) * jnp.exp(lg[None, :] * (n[:, None] + 1.0))[None, :, :, None]
    k_w = k * jnp.exp(lg[None, :] * (c - 1.0 - n)[:, None])[None, :, :, None]
    s_new = s0 * jnp.exp(lg * c)[None, :, None, None] + jnp.einsum('bmhd,bmhv->bhdv', k_w, v)
    return s_new, o


def _retention_prompt(q, k, v):
    bsz, t = q.shape[:2]
    c = min(RET_CHUNK, t)
    nc = t // c

    def chunks(z):
        return jnp.moveaxis(z.reshape((bsz, nc, c) + z.shape[2:]), 1, 0)

    s0 = jnp.zeros((bsz, B_HEADS, B_KEY_DIM, B_VAL_DIM), F32)
    s_fin, o = lax.scan(lambda st, inp: _retention_chunk(st, *inp), s0, (chunks(q), chunks(k), chunks(v)))
    return s_fin, jnp.moveaxis(o, 0, 1).reshape(bsz, t, B_HEADS, B_VAL_DIM)


def _group_norm(o, gain):
    mu = jnp.mean(o, axis=-1, keepdims=True)
    var = jnp.mean(jnp.square(o - mu), axis=-1, keepdims=True)
    on = (o - mu) * lax.rsqrt(var + GN_EPS)
    return on.reshape(o.shape[:2] + (-1,)) * gain.astype(F32)


def _merge_ab(a_o, b_o, bg, gn_gain, w_out):
    b_y = jax.nn.silu(bg.astype(F32)) * _group_norm(b_o, gn_gain)
    h = jnp.concatenate([a_o, b_y], axis=-1)
    return h.astype(w_out.dtype) @ w_out


def _project_fox(x, w_in, b_f):
    bsz, t = x.shape[:2]
    q, k, v, f_logit = _split(x @ w_in, IN_FOX_SPLITS)
    q = q.reshape(bsz, t, C_KV_HEADS, C_GROUP, C_HEAD_DIM)
    k = k.reshape(bsz, t, C_KV_HEADS, C_HEAD_DIM)
    v = v.reshape(bsz, t, C_KV_HEADS, C_HEAD_DIM)
    log_f = jax.nn.log_sigmoid(f_logit.astype(F32) + b_f.astype(F32))
    return q, k, v, log_f


def _heads_first(a):
    bsz, t = a.shape[:2]
    return a.reshape(bsz, t, C_KV_HEADS, C_GROUP).transpose(0, 2, 3, 1)


def _fox_prompt(q, k, v, log_f):
    bsz, t = q.shape[:2]
    nb = t // Q_BLOCK
    c_key = _heads_first(jnp.cumsum(log_f, axis=1))
    q_blocks = jnp.moveaxis(q.reshape(bsz, nb, Q_BLOCK, C_KV_HEADS, C_GROUP, C_HEAD_DIM), 1, 0)
    c_blocks = jnp.moveaxis(c_key.reshape(bsz, C_KV_HEADS, C_GROUP, nb, Q_BLOCK), 3, 0)
    key_pos = jnp.arange(t)
    vf = v.astype(F32)

    def one_block(args):
        qb, cb, n = args
        s = jnp.einsum('bqkgd,bskd->bkgqs', qb, k, preferred_element_type=F32) * (C_HEAD_DIM ** -0.5)
        s = s + cb[..., None] - c_key[:, :, :, None, :]
        q_pos = n * Q_BLOCK + jnp.arange(Q_BLOCK)
        s = jnp.where(key_pos[None, :] <= q_pos[:, None], s, NEG_INF)
        p = jax.nn.softmax(s, axis=-1)
        return jnp.einsum('bkgqs,bskd->bqkgd', p, vf)

    o = lax.map(one_block, (q_blocks, c_blocks, jnp.arange(nb)))
    return jnp.moveaxis(o, 0, 1).reshape(bsz, t, C_Q_W)


def _gather_pages(pool, page_table):
    g = pool[page_table]
    return g.reshape((page_table.shape[0], -1) + pool.shape[2:])


def _fox_sample(q, k_new, v_new, lf_new, k_past, v_past, lf_past):
    bsz, t = q.shape[:2]
    p_len = k_past.shape[1]
    kk = jnp.concatenate([k_past.astype(k_new.dtype), k_new], axis=1)
    vv = jnp.concatenate([v_past.astype(v_new.dtype), v_new], axis=1)
    cum_new = jnp.cumsum(lf_new, axis=1)
    lfp = lf_past.astype(F32)
    r_past = lax.cumsum(lfp, axis=1, reverse=True) - lfp
    key_bias = jnp.concatenate([r_past, -cum_new], axis=1)
    s = jnp.einsum('btkgd,bskd->bkgts', q, kk, preferred_element_type=F32) * (C_HEAD_DIM ** -0.5)
    s = s + _heads_first(cum_new)[..., None] + _heads_first(key_bias)[:, :, :, None, :]
    allowed = jnp.arange(p_len + t)[None, :] <= p_len + jnp.arange(t)[:, None]
    s = jnp.where(allowed, s, NEG_INF)
    pr = jax.nn.softmax(s, axis=-1)
    o = jnp.einsum('bkgts,bskd->btkgd', pr, vv.astype(F32))
    return o.reshape(bsz, t, C_Q_W)


def _mem_kv(mem, w_k, w_v):
    bsz, m = mem.shape[:2]
    k = (mem @ w_k).reshape(bsz, m, MEM_HEADS, MEM_HEAD_DIM)
    v = (mem @ w_v).reshape(bsz, m, MEM_HEADS, MEM_HEAD_DIM)
    return k, v


def _cross_attend(x, k, v, w_q, w_o):
    bsz, t = x.shape[:2]
    q = (x @ w_q).reshape(bsz, t, MEM_HEADS, MEM_HEAD_DIM)
    s = jnp.einsum('bthd,bmhd->bhtm', q, k.astype(q.dtype), preferred_element_type=F32) * (MEM_HEAD_DIM ** -0.5)
    p = jax.nn.softmax(s, axis=-1)
    o = jnp.einsum('bhtm,bmhd->bthd', p, v.astype(F32)).reshape(bsz, t, D_MODEL)
    return o.astype(w_o.dtype) @ w_o


def _hier_moe(x, w_rg, b_rg, w_re, b_re, w_gate, w_up, w_down):
    shp = x.shape
    xt = x.reshape(-1, shp[-1])
    n = xt.shape[0]
    g_logit = (xt @ w_rg).astype(F32) + b_rg.astype(F32)
    g_prob = jax.nn.softmax(g_logit, axis=-1)
    g_onehot = jax.nn.one_hot(jnp.argmax(g_logit, axis=-1), N_GROUPS, dtype=F32)
    g_sel = jnp.max(g_prob, axis=-1, keepdims=True)
    e_logit = ((xt @ w_re).astype(F32) + b_re.astype(F32)).reshape(n, N_GROUPS, EXPERTS_PER_GROUP)
    e_in_g = jnp.einsum('ng,nge->ne', g_onehot, e_logit)
    top_v, top_i = lax.top_k(e_in_g, EXPERT_TOPK)
    w_top = jax.nn.softmax(top_v, axis=-1) * g_sel
    within = jnp.einsum('nk,nke->ne', w_top, jax.nn.one_hot(top_i, EXPERTS_PER_GROUP, dtype=F32))
    comb = (g_onehot[:, :, None] * within[:, None, :]).reshape(n, N_EXPERTS)
    h = jax.nn.silu(jnp.einsum('nd,edf->nef', xt, w_gate)) * jnp.einsum('nd,edf->nef', xt, w_up)
    h = h * comb.astype(h.dtype)[:, :, None]
    y = jnp.einsum('nef,efd->nd', h, w_down)
    return y.astype(x.dtype).reshape(shp)


def setup_inputs(seed: int = 0) -> dict:
    key = jax.random.key(seed)
    ks = jax.random.split(key, 40)
    n_pages = PAST_LEN // PAGE_SIZE
    n_phys = (DEC_BATCH * n_pages * 5) // 4
    d_in = D_MODEL ** -0.5

    def nrm(k, shape, scale=1.0):
        return jax.random.normal(k, shape, F32) * scale

    page_table = jax.random.permutation(ks[12], n_phys)[: DEC_BATCH * n_pages]
    page_table = page_table.reshape(DEC_BATCH, n_pages).astype(jnp.int32)
    return {
        'x_prompt': nrm(ks[0], (BATCH, SEQ, D_MODEL)),
        'x_sample': nrm(ks[1], (DEC_BATCH, DEC_SEQ, D_MODEL)),
        'mem_prompt': nrm(ks[2], (BATCH, N_MEM, D_MODEL)),
        'cache_swa_k': nrm(ks[3], (N_EVEN, DEC_BATCH, WINDOW, A_KV_HEADS, A_HEAD_DIM)),
        'cache_swa_v': nrm(ks[4], (N_EVEN, DEC_BATCH, WINDOW, A_KV_HEADS, A_HEAD_DIM)),
        'state_ret': nrm(ks[5], (N_EVEN, DEC_BATCH, B_HEADS, B_KEY_DIM, B_VAL_DIM)),
        'cache_fox_k': nrm(ks[6], (N_ODD, n_phys, PAGE_SIZE, C_KV_HEADS, C_HEAD_DIM)),
        'cache_fox_v': nrm(ks[7], (N_ODD, n_phys, PAGE_SIZE, C_KV_HEADS, C_HEAD_DIM)),
        'cache_fox_logf': jax.nn.log_sigmoid(FORGET_BIAS_INIT + nrm(ks[8], (N_ODD, n_phys, PAGE_SIZE, C_HEADS))),
        'cache_mem_k': nrm(ks[9], (DEPTH, DEC_BATCH, N_MEM, MEM_HEADS, MEM_HEAD_DIM)),
        'cache_mem_v': nrm(ks[10], (DEPTH, DEC_BATCH, N_MEM, MEM_HEADS, MEM_HEAD_DIM)),
        'page_table': page_table,
        'w_in_ab': nrm(ks[13], (N_EVEN, D_MODEL, IN_AB_W), d_in),
        'sink_ab': nrm(ks[14], (N_EVEN, A_HEADS), 0.5),
        'gn_gain_ab': 1.0 + nrm(ks[15], (N_EVEN, B_V_W), 0.02),
        'w_out_ab': nrm(ks[16], (N_EVEN, MIX_AB_W, D_MODEL), MIX_AB_W ** -0.5 * DEEPNORM_BETA),
        'w_in_fox': nrm(ks[17], (N_ODD, D_MODEL, IN_FOX_W), d_in),
        'b_forget': FORGET_BIAS_INIT + nrm(ks[18], (N_ODD, C_HEADS), 0.1),
        'w_out_fox': nrm(ks[19], (N_ODD, C_Q_W, D_MODEL), C_Q_W ** -0.5 * DEEPNORM_BETA),
        'w_xq': nrm(ks[20], (DEPTH, D_MODEL, D_MODEL), d_in),
        'w_xk': nrm(ks[21], (DEPTH, D_MODEL, D_MODEL), d_in),
        'w_xv': nrm(ks[22], (DEPTH, D_MODEL, D_MODEL), d_in),
        'w_xo': nrm(ks[23], (DEPTH, D_MODEL, D_MODEL), d_in * DEEPNORM_BETA),
        'ln_g': 1.0 + nrm(ks[24], (DEPTH, 3, D_MODEL), 0.02),
        'ln_b': nrm(ks[25], (DEPTH, 3, D_MODEL), 0.02),
        'w_rg': nrm(ks[26], (DEPTH, D_MODEL, N_GROUPS), d_in),
        'b_rg': nrm(ks[27], (DEPTH, N_GROUPS), 0.01),
        'w_re': nrm(ks[28], (DEPTH, D_MODEL, N_EXPERTS), d_in),
        'b_re': nrm(ks[29], (DEPTH, N_EXPERTS), 0.01),
        'w_gate': nrm(ks[30], (DEPTH, N_EXPERTS, D_MODEL, D_EXPERT), d_in),
        'w_up': nrm(ks[31], (DEPTH, N_EXPERTS, D_MODEL, D_EXPERT), d_in),
        'w_down': nrm(ks[32], (DEPTH, N_EXPERTS, D_EXPERT, D_MODEL), D_EXPERT ** -0.5 * DEEPNORM_BETA),
    }


def reference(x_prompt, x_sample, mem_prompt, cache_swa_k, cache_swa_v, state_ret,
              cache_fox_k, cache_fox_v, cache_fox_logf, cache_mem_k, cache_mem_v, page_table,
              w_in_ab, sink_ab, gn_gain_ab, w_out_ab, w_in_fox, b_forget, w_out_fox,
              w_xq, w_xk, w_xv, w_xo, ln_g, ln_b, w_rg, b_rg, w_re, b_re,
              w_gate, w_up, w_down):
    pos_p = jnp.arange(x_prompt.shape[1], dtype=jnp.int32)
    pos_s = PAST_LEN + jnp.arange(x_sample.shape[1], dtype=jnp.int32)
    xp, xs = x_prompt, x_sample
    swa_kp, swa_vp, swa_ks, swa_vs, ret_p, ret_s = [], [], [], [], [], []
    fox_kp, fox_vp, fox_lp, fox_ks, fox_vs, fox_ls = [], [], [], [], [], []
    mem_kp, mem_vp = [], []
    for layer in range(DEPTH):
        i = layer // 2
        if layer % 2 == 0:
            aq, ak, av, bq, bk, bv, bg = _project_ab(xp, w_in_ab[i], pos_p)
            a_o = _swa_prompt(aq, ak, av, sink_ab[i])
            s_fin, b_o = _retention_prompt(bq, bk, bv)
            hp = _merge_ab(a_o, b_o, bg, gn_gain_ab[i], w_out_ab[i])
            swa_kp.append(ak[:, -WINDOW:])
            swa_vp.append(av[:, -WINDOW:])
            ret_p.append(s_fin.astype(xp.dtype))
            aq, ak, av, bq, bk, bv, bg = _project_ab(xs, w_in_ab[i], pos_s)
            a_o, k_buf, v_buf = _swa_sample(aq, ak, av, cache_swa_k[i], cache_swa_v[i], sink_ab[i])
            s_new, b_o = _retention_chunk(state_ret[i].astype(F32), bq, bk, bv)
            hs = _merge_ab(a_o, b_o, bg, gn_gain_ab[i], w_out_ab[i])
            swa_ks.append(k_buf)
            swa_vs.append(v_buf)
            ret_s.append(s_new.astype(xs.dtype))
        else:
            q, k, v, lf = _project_fox(xp, w_in_fox[i], b_forget[i])
            hp = _fox_prompt(q, k, v, lf).astype(w_out_fox.dtype) @ w_out_fox[i]
            fox_kp.append(k)
            fox_vp.append(v)
            fox_lp.append(lf.astype(xp.dtype))
            q, k, v, lf = _project_fox(xs, w_in_fox[i], b_forget[i])
            k_past = _gather_pages(cache_fox_k[i], page_table)
            v_past = _gather_pages(cache_fox_v[i], page_table)
            lf_past = _gather_pages(cache_fox_logf[i], page_table)
            hs = _fox_sample(q, k, v, lf, k_past, v_past, lf_past).astype(w_out_fox.dtype) @ w_out_fox[i]
            fox_ks.append(k)
            fox_vs.append(v)
            fox_ls.append(lf.astype(xs.dtype))
        xp = _post_norm(xp, hp, ln_g[layer, 0], ln_b[layer, 0])
        xs = _post_norm(xs, hs, ln_g[layer, 0], ln_b[layer, 0])
        mk, mv = _mem_kv(mem_prompt, w_xk[layer], w_xv[layer])
        mem_kp.append(mk)
        mem_vp.append(mv)
        xp = _post_norm(xp, _cross_attend(xp, mk, mv, w_xq[layer], w_xo[layer]), ln_g[layer, 1], ln_b[layer, 1])
        xs = _post_norm(xs, _cross_attend(xs, cache_mem_k[layer], cache_mem_v[layer], w_xq[layer], w_xo[layer]),
                        ln_g[layer, 1], ln_b[layer, 1])
        moe_w = (w_rg[layer], b_rg[layer], w_re[layer], b_re[layer], w_gate[layer], w_up[layer], w_down[layer])
        xp = _post_norm(xp, _hier_moe(xp, *moe_w), ln_g[layer, 2], ln_b[layer, 2])
        xs = _post_norm(xs, _hier_moe(xs, *moe_w), ln_g[layer, 2], ln_b[layer, 2])
    swa_k_prompt = jnp.stack(swa_kp)
    swa_v_prompt = jnp.stack(swa_vp)
    swa_k_sample = jnp.stack(swa_ks)
    swa_v_sample = jnp.stack(swa_vs)
    ret_prompt = jnp.stack(ret_p)
    ret_sample = jnp.stack(ret_s)
    fox_k_prompt = jnp.stack(fox_kp)
    fox_v_prompt = jnp.stack(fox_vp)
    fox_logf_prompt = jnp.stack(fox_lp)
    fox_k_sample = jnp.stack(fox_ks)
    fox_v_sample = jnp.stack(fox_vs)
    fox_logf_sample = jnp.stack(fox_ls)
    mem_k_prompt = jnp.stack(mem_kp)
    mem_v_prompt = jnp.stack(mem_vp)
    return (xp, xs, swa_k_prompt, swa_v_prompt, swa_k_sample, swa_v_sample, ret_prompt, ret_sample,
            fox_k_prompt, fox_v_prompt, fox_logf_prompt, fox_k_sample, fox_v_sample, fox_logf_sample,
            mem_k_prompt, mem_v_prompt)
```

```python
import functools

import numpy as np
import jax
import jax.numpy as jnp
from jax import lax
from jax.experimental import pallas as pl
from jax.experimental.pallas import tpu as pltpu

F32 = jnp.float32
BF16 = jnp.bfloat16

D_MODEL = 1024
DEPTH = 2
PAGE_SIZE = 128
A_HEADS = 8
A_KV_HEADS = 2
A_GROUP = A_HEADS // A_KV_HEADS
A_HEAD_DIM = 64
WINDOW = 128
B_HEADS = 4
B_KEY_DIM = 128
B_VAL_DIM = 128
RET_CHUNK = 128
ROPE_BASE = 10000.0
C_HEADS = 16
C_KV_HEADS = 4
C_GROUP = C_HEADS // C_KV_HEADS
C_HEAD_DIM = 64
MEM_HEADS = 4
MEM_HEAD_DIM = D_MODEL // MEM_HEADS
N_GROUPS = 4
EXPERTS_PER_GROUP = 4
N_EXPERTS = N_GROUPS * EXPERTS_PER_GROUP
D_EXPERT = D_MODEL // 2
LN_EPS = 1e-5
GN_EPS = 1e-6
DEEPNORM_ALPHA = (2 * DEPTH) ** 0.25
NEG_INF = -1e30

A_Q_W = A_HEADS * A_HEAD_DIM
A_KV_W = A_KV_HEADS * A_HEAD_DIM
B_W = B_HEADS * B_KEY_DIM
C_Q_W = C_HEADS * C_HEAD_DIM
C_KV_W = C_KV_HEADS * C_HEAD_DIM
IN_FOX_PAD_W = C_Q_W + 2 * C_KV_W + 128

LANES = 128
TOKEN_BLOCK = 512
MOE_TILE = 256
DECODE_PAGES_PER_CHUNK = 8
VMEM_LIMIT = 56 * 1024 * 1024

_NT = (((1,), (1,)), ((), ()))
_TN = (((0,), (0,)), ((), ()))


def _params(*sem):
    return pltpu.CompilerParams(dimension_semantics=sem, vmem_limit_bytes=VMEM_LIMIT)


def _dot(a, b):
    return jnp.dot(a, b, preferred_element_type=F32)


def _dot_nt(a, b):
    return lax.dot_general(a, b, _NT, preferred_element_type=F32)


def _split3(x):
    hi = x.astype(BF16)
    r1 = x - hi.astype(F32)
    mid = r1.astype(BF16)
    lo = (r1 - mid.astype(F32)).astype(BF16)
    return hi, mid, lo


def _layer_norm(y, g, b):
    mu = jnp.mean(y, axis=-1, keepdims=True)
    d = y - mu
    var = jnp.mean(d * d, axis=-1, keepdims=True)
    return d * lax.rsqrt(var + LN_EPS) * g + b


def _sigmoid(x):
    return 1.0 / (1.0 + jnp.exp(-x))


def _mm_kernel(x_ref, w_ref, o_ref):
    o_ref[...] = _dot(x_ref[...].astype(BF16), w_ref[...]).astype(o_ref.dtype)


def _matmul(x, w, out_dtype=F32):
    m, k = x.shape
    n = w.shape[1]
    bm = min(TOKEN_BLOCK, m)
    return pl.pallas_call(
        _mm_kernel,
        grid=(m // bm,),
        in_specs=[pl.BlockSpec((bm, k), lambda i: (i, 0)),
                  pl.BlockSpec((k, n), lambda i: (0, 0))],
        out_specs=pl.BlockSpec((bm, n), lambda i: (i, 0)),
        out_shape=jax.ShapeDtypeStruct((m, n), out_dtype),
        compiler_params=_params("arbitrary"),
        name="matmul",
    )(x, w)


def _proj_norm_kernel(*refs, n_in):
    h_refs, w_refs = refs[:n_in], refs[n_in:2 * n_in]
    x_ref, g_ref, b_ref, o_ref = refs[2 * n_in:]
    acc = _dot(h_refs[0][...].astype(BF16), w_refs[0][...])
    for h_ref, w_ref in zip(h_refs[1:], w_refs[1:]):
        acc = acc + _dot(h_ref[...].astype(BF16), w_ref[...])
    o_ref[...] = _layer_norm(DEEPNORM_ALPHA * x_ref[...] + acc, g_ref[...], b_ref[...])


def _proj_norm(hs, ws, x, g, b):
    m, d = x.shape
    bm = min(TOKEN_BLOCK, m)
    n_in = len(hs)
    in_specs = ([pl.BlockSpec((bm, h.shape[1]), lambda i: (i, 0)) for h in hs]
                + [pl.BlockSpec(w.shape, lambda i: (0, 0)) for w in ws]
                + [pl.BlockSpec((bm, d), lambda i: (i, 0)),
                   pl.BlockSpec((1, d), lambda i: (0, 0)),
                   pl.BlockSpec((1, d), lambda i: (0, 0))])
    return pl.pallas_call(
        functools.partial(_proj_norm_kernel, n_in=n_in),
        grid=(m // bm,),
        in_specs=in_specs,
        out_specs=pl.BlockSpec((bm, d), lambda i: (i, 0)),
        out_shape=jax.ShapeDtypeStruct((m, d), F32),
        compiler_params=_params("arbitrary"),
        name="proj_norm",
    )(*hs, *ws, x, g.reshape(1, d), b.reshape(1, d))


_AB_Q, _AB_BQ, _AB_BK, _AB_BV, _AB_BG = 0, 512, 1024, 1536, 2048
_AB_AK, _AB_AV = 2560, 2688
_AB_W = 2816


def _permute_w_in_ab(w):
    aq, ak, av, bq, bk, bv, bg = jnp.split(
        w, np.cumsum([A_Q_W, A_KV_W, A_KV_W, B_W, B_W, B_W]).tolist(), axis=-1)
    return jnp.concatenate([aq, bq, bk, bv, bg, ak, av], axis=-1)


def _sink_softmax_pv(s, valid, sk, v):
    s = jnp.where(valid, s, NEG_INF)
    m = jnp.maximum(jnp.max(s, axis=-1, keepdims=True), sk)
    p = jnp.exp(s - m)
    den = jnp.sum(p, axis=-1, keepdims=True) + jnp.exp(sk - m)
    return _dot(p.astype(BF16), v) / den


def _sink_column(sink_ref, kvh, rows_per_head):
    n = A_GROUP * rows_per_head
    grp = lax.broadcasted_iota(jnp.int32, (n, 1), 0) // rows_per_head
    sk = jnp.full((n, 1), sink_ref[kvh * A_GROUP], F32)
    for g in range(1, A_GROUP):
        sk = jnp.where(grp == g, sink_ref[kvh * A_GROUP + g], sk)
    return sk


def _swa_prompt_kernel(sink_ref, q_ref, kc_ref, vc_ref, kp_ref, vp_ref, o_ref, *, tq, blocks_per_seq):
    i = pl.program_id(0)
    lo = jnp.where(i % blocks_per_seq == 0, WINDOW, 0)
    kk = jnp.concatenate([kp_ref[...], kc_ref[...]], axis=0).astype(BF16)
    vv = jnp.concatenate([vp_ref[...], vc_ref[...]], axis=0).astype(BF16)
    n = A_GROUP * WINDOW
    qi = lax.broadcasted_iota(jnp.int32, (n, 2 * WINDOW), 0) % WINDOW
    kj = lax.broadcasted_iota(jnp.int32, (n, 2 * WINDOW), 1)
    band = (kj >= qi) & (kj <= qi + WINDOW)
    band0 = band & (kj >= lo)
    for j in range(tq // WINDOW):
        q = q_ref[j * WINDOW:(j + 1) * WINDOW, :]
        valid = band0 if j == 0 else band
        for kvh in range(A_KV_HEADS):
            heads = [q[:, (kvh * A_GROUP + g) * A_HEAD_DIM:(kvh * A_GROUP + g + 1) * A_HEAD_DIM]
                     for g in range(A_GROUP)]
            q4 = (jnp.concatenate(heads, axis=0) * (A_HEAD_DIM ** -0.5)).astype(BF16)
            ksl = kk[j * WINDOW:(j + 2) * WINDOW, kvh * A_HEAD_DIM:(kvh + 1) * A_HEAD_DIM]
            vsl = vv[j * WINDOW:(j + 2) * WINDOW, kvh * A_HEAD_DIM:(kvh + 1) * A_HEAD_DIM]
            o = _sink_softmax_pv(_dot_nt(q4, ksl), valid, _sink_column(sink_ref, kvh, WINDOW), vsl)
            for g in range(A_GROUP):
                c0 = (kvh * A_GROUP + g) * A_HEAD_DIM
                o_ref[j * WINDOW:(j + 1) * WINDOW, c0:c0 + A_HEAD_DIM] = o[g * WINDOW:(g + 1) * WINDOW]


def _swa_prompt(z, sink, seq):
    n = z.shape[0]
    tq = TOKEN_BLOCK
    sub = tq // WINDOW
    kcol, vcol = _AB_AK // LANES, _AB_AV // LANES
    return pl.pallas_call(
        functools.partial(_swa_prompt_kernel, tq=tq, blocks_per_seq=seq // tq),
        grid=(n // tq,),
        in_specs=[pl.BlockSpec(memory_space=pltpu.SMEM),
                  pl.BlockSpec((tq, A_Q_W), lambda i: (i, 0)),
                  pl.BlockSpec((tq, LANES), lambda i: (i, kcol)),
                  pl.BlockSpec((tq, LANES), lambda i: (i, vcol)),
                  pl.BlockSpec((WINDOW, LANES), lambda i: (jnp.maximum(i * sub - 1, 0), kcol)),
                  pl.BlockSpec((WINDOW, LANES), lambda i: (jnp.maximum(i * sub - 1, 0), vcol))],
        out_specs=pl.BlockSpec((tq, A_Q_W), lambda i: (i, 0)),
        out_shape=jax.ShapeDtypeStruct((n, A_Q_W), F32),
        compiler_params=_params("arbitrary"),
        name="swa_prompt",
    )(sink, z, z, z, z, z)


def _rope_tables(pos):
    half = B_KEY_DIM // 2
    inv_freq = 1.0 / (ROPE_BASE ** jnp.linspace(0.0, 1.0, half, dtype=F32))
    ang = pos.astype(F32)[:, None] * inv_freq[None, :]
    cos, sin = jnp.cos(ang), jnp.sin(ang)
    return jnp.concatenate([cos, cos], axis=-1), jnp.concatenate([-sin, sin], axis=-1)


def _rope(x, cos2, sin2):
    return x * cos2 + pltpu.roll(x, B_KEY_DIM // 2, 1) * sin2


def _ret_log_gamma():
    return jnp.log(1.0 - 2.0 ** (-5.0 - jnp.arange(B_HEADS, dtype=F32)))


def _group_norm_gate(o, gate, gain):
    mu = jnp.mean(o, axis=-1, keepdims=True)
    d = o - mu
    var = jnp.mean(d * d, axis=-1, keepdims=True)
    return gate * _sigmoid(gate) * (d * lax.rsqrt(var + GN_EPS) * gain)


def _ret_prompt_kernel(lg_ref, q_ref, k_ref, v_ref, g_ref, cos_ref, sin_ref, gain_ref,
                       y_ref, ret_ref, s_ref):
    h = pl.program_id(0)
    c = pl.program_id(1)
    n = RET_CHUNK

    @pl.when(c == 0)
    def _():
        s_ref[...] = jnp.zeros_like(s_ref)

    lg = lg_ref[h]
    r = lax.broadcasted_iota(jnp.int32, (n, n), 0)
    col = lax.broadcasted_iota(jnp.int32, (n, n), 1)
    diff = (r - col).astype(F32)
    decay = jnp.where(diff >= 0, jnp.exp(lg * jnp.maximum(diff, 0.0)), 0.0)
    rf = r.astype(F32)
    e_q = jnp.exp(lg * (rf + 1.0))
    e_k = jnp.exp(lg * (n - 1.0 - rf))
    e_c = jnp.exp(lg * jnp.full((1, n), float(n), F32))
    cos2, sin2 = cos_ref[...], sin_ref[...]
    gain = gain_ref[...]
    for b in range(q_ref.shape[0]):
        q = _rope(q_ref[b], cos2, sin2)
        k = _rope(k_ref[b], cos2, sin2) * (B_KEY_DIM ** -0.5)
        v = v_ref[b].astype(BF16)
        s0 = s_ref[b]
        qb = q.astype(BF16)
        inner = _dot_nt(qb, k.astype(BF16)) * decay
        o = _dot(inner.astype(BF16), v) + _dot(qb, s0.astype(BF16)) * e_q
        kw = (k * e_k).astype(BF16)
        s_ref[b] = s0 * e_c + lax.dot_general(kw, v, _TN, preferred_element_type=F32)
        y_ref[b] = _group_norm_gate(o, g_ref[b], gain)

    @pl.when(c == pl.num_programs(1) - 1)
    def _():
        for b in range(q_ref.shape[0]):
            ret_ref[b, 0] = s_ref[b]


def _ret_prompt(z3, cos2, sin2, gain):
    bsz, t, _ = z3.shape
    n = RET_CHUNK

    def col(base):
        return lambda h, c: (0, c, base // LANES + h)

    return pl.pallas_call(
        _ret_prompt_kernel,
        grid=(B_HEADS, t // n),
        in_specs=[pl.BlockSpec(memory_space=pltpu.SMEM),
                  pl.BlockSpec((bsz, n, LANES), col(_AB_BQ)),
                  pl.BlockSpec((bsz, n, LANES), col(_AB_BK)),
                  pl.BlockSpec((bsz, n, LANES), col(_AB_BV)),
                  pl.BlockSpec((bsz, n, LANES), col(_AB_BG)),
                  pl.BlockSpec((n, LANES), lambda h, c: (c, 0)),
                  pl.BlockSpec((n, LANES), lambda h, c: (c, 0)),
                  pl.BlockSpec((1, LANES), lambda h, c: (0, h))],
        out_specs=[pl.BlockSpec((bsz, n, LANES), lambda h, c: (0, c, h)),
                   pl.BlockSpec((bsz, 1, B_KEY_DIM, B_VAL_DIM), lambda h, c: (0, h, 0, 0))],
        out_shape=[jax.ShapeDtypeStruct((bsz, t, B_W), F32),
                   jax.ShapeDtypeStruct((bsz, B_HEADS, B_KEY_DIM, B_VAL_DIM), F32)],
        scratch_shapes=[pltpu.VMEM((bsz, B_KEY_DIM, B_VAL_DIM), F32)],
        compiler_params=_params("arbitrary", "arbitrary"),
        name="ret_prompt",
    )(_ret_log_gamma(), z3, z3, z3, z3, cos2, sin2, gain.reshape(1, B_W))


def _ab_sample_kernel(sink_ref, lg_ref, z_ref, ck_ref, cv_ref, st_ref, cos_ref, sin_ref, gain_ref,
                      h_ref, kb_ref, vb_ref, sn_ref, *, t):
    w = WINDOW
    n = A_GROUP * t
    pad = jnp.zeros((w - t, LANES), F32)
    ti = lax.broadcasted_iota(jnp.int32, (n, 2 * w), 0) % t
    kj = lax.broadcasted_iota(jnp.int32, (n, 2 * w), 1)
    valid = (kj >= ti) & (kj <= ti + w)
    cos2, sin2 = cos_ref[...], sin_ref[...]
    r = lax.broadcasted_iota(jnp.int32, (w, w), 0)
    col = lax.broadcasted_iota(jnp.int32, (w, w), 1)
    diff = (r - col).astype(F32)
    rf = r.astype(F32)
    live = (r < t) & (col < t)
    for b in range(z_ref.shape[0]):
        z = z_ref[b]
        k_new = z[:, _AB_AK:_AB_AK + A_KV_W]
        v_new = z[:, _AB_AV:_AB_AV + A_KV_W]
        kk = jnp.concatenate([ck_ref[b], k_new, pad], axis=0)
        vv = jnp.concatenate([cv_ref[b], v_new, pad], axis=0)
        kb_ref[b] = kk[t:t + w]
        vb_ref[b] = vv[t:t + w]
        kkb, vvb = kk.astype(BF16), vv.astype(BF16)
        a_parts = []
        for kvh in range(A_KV_HEADS):
            heads = [z[:, (kvh * A_GROUP + g) * A_HEAD_DIM:(kvh * A_GROUP + g + 1) * A_HEAD_DIM]
                     for g in range(A_GROUP)]
            q4 = (jnp.concatenate(heads, axis=0) * (A_HEAD_DIM ** -0.5)).astype(BF16)
            ksl = kkb[:, kvh * A_HEAD_DIM:(kvh + 1) * A_HEAD_DIM]
            vsl = vvb[:, kvh * A_HEAD_DIM:(kvh + 1) * A_HEAD_DIM]
            o = _sink_softmax_pv(_dot_nt(q4, ksl), valid, _sink_column(sink_ref, kvh, t), vsl)
            a_parts += [o[g * t:(g + 1) * t] for g in range(A_GROUP)]
        b_parts = []
        for hd in range(B_HEADS):
            lg = lg_ref[hd]
            decay = jnp.where(live & (diff >= 0), jnp.exp(lg * jnp.maximum(diff, 0.0)), 0.0)
            e_q = jnp.exp(lg * (rf + 1.0))
            e_k = jnp.where(r < t, jnp.exp(lg * (t - 1.0 - rf)), 0.0)
            e_c = jnp.exp(lg * jnp.full((1, w), float(t), F32))
            sl = slice(hd * B_KEY_DIM, (hd + 1) * B_KEY_DIM)
            zq = jnp.concatenate([z[:, _AB_BQ + hd * B_KEY_DIM:_AB_BQ + (hd + 1) * B_KEY_DIM], pad], axis=0)
            zk = jnp.concatenate([z[:, _AB_BK + hd * B_KEY_DIM:_AB_BK + (hd + 1) * B_KEY_DIM], pad], axis=0)
            zv = jnp.concatenate([z[:, _AB_BV + hd * B_VAL_DIM:_AB_BV + (hd + 1) * B_VAL_DIM], pad], axis=0)
            zg = z[:, _AB_BG + hd * B_VAL_DIM:_AB_BG + (hd + 1) * B_VAL_DIM]
            q = _rope(zq, cos2, sin2)
            k = _rope(zk, cos2, sin2) * (B_KEY_DIM ** -0.5)
            v = zv.astype(BF16)
            s0 = st_ref[b, hd]
            qb = q.astype(BF16)
            inner = _dot_nt(qb, k.astype(BF16)) * decay
            o = _dot(inner.astype(BF16), v) + _dot(qb, s0.astype(BF16)) * e_q
            kw = (k * e_k).astype(BF16)
            sn_ref[b, hd] = s0 * e_c + lax.dot_general(kw, v, _TN, preferred_element_type=F32)
            b_parts.append(_group_norm_gate(o[:t], zg, gain_ref[:, sl]))
        h_ref[b] = jnp.concatenate(a_parts + b_parts, axis=-1)


def _ab_sample(z3, cache_k, cache_v, state, sink, cos2, sin2, gain):
    db, t, _ = z3.shape
    bb = 4
    w = WINDOW
    cos_p = jnp.concatenate([cos2, jnp.ones((w - t, LANES), F32)], axis=0)
    sin_p = jnp.concatenate([sin2, jnp.zeros((w - t, LANES), F32)], axis=0)
    smem = pl.BlockSpec(memory_space=pltpu.SMEM)
    return pl.pallas_call(
        functools.partial(_ab_sample_kernel, t=t),
        grid=(db // bb,),
        in_specs=[smem, smem,
                  pl.BlockSpec((bb, t, _AB_W), lambda i: (i, 0, 0)),
                  pl.BlockSpec((bb, w, A_KV_W), lambda i: (i, 0, 0)),
                  pl.BlockSpec((bb, w, A_KV_W), lambda i: (i, 0, 0)),
                  pl.BlockSpec((bb, B_HEADS, B_KEY_DIM, B_VAL_DIM), lambda i: (i, 0, 0, 0)),
                  pl.BlockSpec((w, LANES), lambda i: (0, 0)),
                  pl.BlockSpec((w, LANES), lambda i: (0, 0)),
                  pl.BlockSpec((1, B_W), lambda i: (0, 0))],
        out_specs=[pl.BlockSpec((bb, t, A_Q_W + B_W), lambda i: (i, 0, 0)),
                   pl.BlockSpec((bb, w, A_KV_W), lambda i: (i, 0, 0)),
                   pl.BlockSpec((bb, w, A_KV_W), lambda i: (i, 0, 0)),
                   pl.BlockSpec((bb, B_HEADS, B_KEY_DIM, B_VAL_DIM), lambda i: (i, 0, 0, 0))],
        out_shape=[jax.ShapeDtypeStruct((db, t, A_Q_W + B_W), F32),
                   jax.ShapeDtypeStruct((db, w, A_KV_W), F32),
                   jax.ShapeDtypeStruct((db, w, A_KV_W), F32),
                   jax.ShapeDtypeStruct((db, B_HEADS, B_KEY_DIM, B_VAL_DIM), F32)],
        compiler_params=_params("arbitrary"),
        name="ab_sample",
    )(sink, _ret_log_gamma(), z3, cache_k, cache_v, state, cos_p, sin_p, gain.reshape(1, B_W))


def _softmax_pv(s, v):
    m = jnp.max(s, axis=-1, keepdims=True)
    p = jnp.exp(s - m)
    return _dot(p.astype(BF16), v) / jnp.sum(p, axis=-1, keepdims=True)


def _xattn_prompt_kernel(x_ref, wq_ref, wo_ref, mk_ref, mv_ref, g_ref, b_ref, o_ref):
    x = x_ref[...]
    q = _dot(x.astype(BF16), wq_ref[...])
    outs = []
    for h in range(MEM_HEADS):
        sl = slice(h * MEM_HEAD_DIM, (h + 1) * MEM_HEAD_DIM)
        qh = (q[:, sl] * (MEM_HEAD_DIM ** -0.5)).astype(BF16)
        s = _dot_nt(qh, mk_ref[:, sl].astype(BF16))
        outs.append(_softmax_pv(s, mv_ref[:, sl].astype(BF16)).astype(BF16))
    y = DEEPNORM_ALPHA * x + _dot(jnp.concatenate(outs, axis=-1), wo_ref[...])
    o_ref[...] = _layer_norm(y, g_ref[...], b_ref[...])


def _xattn_prompt(x, mk, mv, wq, wo, g, b, bsz):
    n, d = x.shape
    bm = TOKEN_BLOCK
    per = n // bsz // bm
    n_mem = mk.shape[0] // bsz
    full = lambda bi, i: (0, 0)
    return pl.pallas_call(
        _xattn_prompt_kernel,
        grid=(bsz, per),
        in_specs=[pl.BlockSpec((bm, d), lambda bi, i: (bi * per + i, 0)),
                  pl.BlockSpec((d, d), full),
                  pl.BlockSpec((d, d), full),
                  pl.BlockSpec((n_mem, d), lambda bi, i: (bi, 0)),
                  pl.BlockSpec((n_mem, d), lambda bi, i: (bi, 0)),
                  pl.BlockSpec((1, d), full),
                  pl.BlockSpec((1, d), full)],
        out_specs=pl.BlockSpec((bm, d), lambda bi, i: (bi * per + i, 0)),
        out_shape=jax.ShapeDtypeStruct((n, d), F32),
        compiler_params=_params("arbitrary", "arbitrary"),
        name="xattn_prompt",
    )(x, wq, wo, mk, mv, g.reshape(1, d), b.reshape(1, d))


def _xattn_sample_kernel(q_ref, k_ref, v_ref, o_ref, *, t):
    pad = jnp.zeros((8 - t, D_MODEL), F32)
    for b in range(q_ref.shape[0]):
        q = jnp.concatenate([q_ref[b], pad], axis=0)
        outs = []
        for h in range(MEM_HEADS):
            sl = slice(h * MEM_HEAD_DIM, (h + 1) * MEM_HEAD_DIM)
            qh = (q[:, sl] * (MEM_HEAD_DIM ** -0.5)).astype(BF16)
            s = _dot_nt(qh, k_ref[b, :, sl].astype(BF16))
            outs.append(_softmax_pv(s, v_ref[b, :, sl].astype(BF16)))
        o_ref[b] = jnp.concatenate(outs, axis=-1)[:t]


def _xattn_sample(q3, ck, cv):
    db, t, d = q3.shape
    n_mem = ck.shape[1]
    bb = 4
    return pl.pallas_call(
        functools.partial(_xattn_sample_kernel, t=t),
        grid=(db // bb,),
        in_specs=[pl.BlockSpec((bb, t, d), lambda i: (i, 0, 0)),
                  pl.BlockSpec((bb, n_mem, d), lambda i: (i, 0, 0)),
                  pl.BlockSpec((bb, n_mem, d), lambda i: (i, 0, 0))],
        out_specs=pl.BlockSpec((bb, t, d), lambda i: (i, 0, 0)),
        out_shape=jax.ShapeDtypeStruct((db, t, d), F32),
        compiler_params=_params("arbitrary"),
        name="xattn_sample",
    )(q3, ck, cv)


def _log_sigmoid(x):
    return -(jnp.maximum(-x, 0.0) + jnp.log1p(jnp.exp(-jnp.abs(x))))


def _logf_kernel(f_ref, bf_ref, lf_ref, c_ref, ct_ref, carry_ref, *, seg):
    i = pl.program_id(0)
    n = LANES
    lf = _log_sigmoid(f_ref[...] + bf_ref[...])
    r = lax.broadcasted_iota(jnp.int32, (n, n), 0)
    col = lax.broadcasted_iota(jnp.int32, (n, n), 1)
    if seg >= n:
        tri = col <= r
    else:
        tri = (col <= r) & (col // seg == r // seg)
    tri = jnp.where(tri, 1.0, 0.0).astype(BF16)
    hi, mid, lo = _split3(lf)
    c = _dot(tri, hi) + _dot(tri, mid) + _dot(tri, lo)
    if seg >= n:
        blocks_per_seg = seg // n

        @pl.when(i % blocks_per_seg == 0)
        def _():
            carry_ref[...] = jnp.zeros_like(carry_ref)

        c = c + carry_ref[0:1, :]
        carry_ref[...] = jnp.broadcast_to(c[n - 1:n, :], carry_ref.shape)
    lf_ref[...] = lf[:, :C_HEADS]
    c_ref[...] = c[:, :C_HEADS]
    ct_ref[...] = c.T[:C_HEADS, :]


def _logf_cumsum(zf, b_f, seg):
    n = zf.shape[0]
    fcol = (C_Q_W + 2 * C_KV_W) // LANES
    bf = jnp.concatenate([b_f.astype(F32), jnp.zeros((LANES - C_HEADS,), F32)]).reshape(1, LANES)
    return pl.pallas_call(
        functools.partial(_logf_kernel, seg=seg),
        grid=(n // LANES,),
        in_specs=[pl.BlockSpec((LANES, LANES), lambda i: (i, fcol)),
                  pl.BlockSpec((1, LANES), lambda i: (0, 0))],
        out_specs=[pl.BlockSpec((LANES, C_HEADS), lambda i: (i, 0)),
                   pl.BlockSpec((LANES, C_HEADS), lambda i: (i, 0)),
                   pl.BlockSpec((C_HEADS, LANES), lambda i: (0, i))],
        out_shape=[jax.ShapeDtypeStruct((n, C_HEADS), F32),
                   jax.ShapeDtypeStruct((n, C_HEADS), F32),
                   jax.ShapeDtypeStruct((C_HEADS, n), F32)],
        scratch_shapes=[pltpu.VMEM((8, LANES), F32)],
        compiler_params=_params("arbitrary"),
        name="logf_cumsum",
    )(zf, bf)


def _fox_prompt_kernel(qi_ref, ki_ref, q_ref, k_ref, v_ref, cq_ref, ck_ref, o_ref,
                       q4_s, cq_s, m_s, l_s, acc_s, *, tq):
    kvh = pl.program_id(0) % C_KV_HEADS
    p = pl.program_id(1)
    qi = qi_ref[p]
    ki = ki_ref[p]
    n = C_GROUP * tq
    d = C_HEAD_DIM

    @pl.when(ki == 0)
    def _():
        q = q_ref[...]
        q4 = jnp.concatenate([q[:, g * d:(g + 1) * d] for g in range(C_GROUP)], axis=0)
        q4_s[...] = (q4 * (d ** -0.5)).astype(BF16)
        cq = cq_ref[0, 0]
        cq_s[...] = jnp.concatenate([cq[:, g:g + 1] for g in range(C_GROUP)], axis=0)
        m_s[...] = jnp.full_like(m_s, NEG_INF)
        l_s[...] = jnp.zeros_like(l_s)
        acc_s[...] = jnp.zeros_like(acc_s)

    odd = kvh % 2 == 1
    kblk = k_ref[...]
    vblk = v_ref[...]
    k = jnp.where(odd, kblk[:, d:], kblk[:, :d]).astype(BF16)
    v = jnp.where(odd, vblk[:, d:], vblk[:, :d]).astype(BF16)
    s = _dot_nt(q4_s[...], k)
    ck = ck_ref[0, 0]
    ckf = jnp.concatenate([jnp.broadcast_to(ck[g:g + 1, :], (tq, tq)) for g in range(C_GROUP)], axis=0)
    s = s + cq_s[...] - ckf
    q_pos = qi * tq + lax.broadcasted_iota(jnp.int32, (n, tq), 0) % tq
    k_pos = ki * tq + lax.broadcasted_iota(jnp.int32, (n, tq), 1)
    s = jnp.where(k_pos <= q_pos, s, NEG_INF)
    m_old = m_s[...]
    m_new = jnp.maximum(m_old, jnp.max(s, axis=-1, keepdims=True))
    a = jnp.exp(m_old - m_new)
    pr = jnp.exp(s - m_new)
    l_s[...] = a * l_s[...] + jnp.sum(pr, axis=-1, keepdims=True)
    acc_s[...] = a * acc_s[...] + _dot(pr.astype(BF16), v)
    m_s[...] = m_new

    @pl.when(ki == qi)
    def _():
        o = acc_s[...] / l_s[...]
        o_ref[...] = jnp.concatenate([o[g * tq:(g + 1) * tq] for g in range(C_GROUP)], axis=-1)


def _fox_prompt(zf, c_col4, c_row4, bsz, seq):
    n = zf.shape[0]
    tq = TOKEN_BLOCK
    nq = seq // tq
    pairs = [(q, k) for q in range(nq) for k in range(q + 1)]
    qi = jnp.asarray([p[0] for p in pairs], jnp.int32)
    ki = jnp.asarray([p[1] for p in pairs], jnp.int32)
    kcol, vcol = C_Q_W // LANES, (C_Q_W + C_KV_W) // LANES
    qw = C_GROUP * C_HEAD_DIM

    def qmap(bk, p, qi, ki):
        return ((bk // C_KV_HEADS) * nq + qi[p], bk % C_KV_HEADS)

    def kmap(base):
        return lambda bk, p, qi, ki: ((bk // C_KV_HEADS) * nq + ki[p], base + (bk % C_KV_HEADS) // 2)

    grid_spec = pltpu.PrefetchScalarGridSpec(
        num_scalar_prefetch=2,
        grid=(bsz * C_KV_HEADS, len(pairs)),
        in_specs=[pl.BlockSpec((tq, qw), qmap),
                  pl.BlockSpec((tq, LANES), kmap(kcol)),
                  pl.BlockSpec((tq, LANES), kmap(vcol)),
                  pl.BlockSpec((1, 1, tq, C_GROUP),
                               lambda bk, p, qi, ki: (bk // C_KV_HEADS, bk % C_KV_HEADS, qi[p], 0)),
                  pl.BlockSpec((1, 1, C_GROUP, tq),
                               lambda bk, p, qi, ki: (bk // C_KV_HEADS, bk % C_KV_HEADS, 0, ki[p]))],
        out_specs=pl.BlockSpec((tq, qw), qmap),
        scratch_shapes=[pltpu.VMEM((C_GROUP * tq, C_HEAD_DIM), BF16),
                        pltpu.VMEM((C_GROUP * tq, 1), F32),
                        pltpu.VMEM((C_GROUP * tq, 1), F32),
                        pltpu.VMEM((C_GROUP * tq, 1), F32),
                        pltpu.VMEM((C_GROUP * tq, C_HEAD_DIM), F32)])
    return pl.pallas_call(
        functools.partial(_fox_prompt_kernel, tq=tq),
        grid_spec=grid_spec,
        out_shape=jax.ShapeDtypeStruct((n, C_Q_W), F32),
        compiler_params=_params("arbitrary", "arbitrary"),
        name="fox_prompt",
    )(qi, ki, zf, zf, zf, c_col4, c_row4)


def _fox_decode_kernel(pt_ref, qbd_ref, cn_ref, cnk_ref, kn_ref, vn_ref, tri_ref,
                       kpool, vpool, lpool, o_ref, kbuf, vbuf, lbuf, sem, *, n_pages, chunk, t):
    b = pl.program_id(0)
    nb = pl.num_programs(0)
    n_chunks = n_pages // chunk
    rows = t * C_HEADS
    ck = chunk * PAGE_SIZE

    def copies(bb, c, slot):
        out = []
        for g in range(chunk):
            page = pt_ref[bb, n_pages - 1 - (c * chunk + g)]
            out.append(pltpu.make_async_copy(kpool.at[page], kbuf.at[slot, g], sem.at[slot]))
            out.append(pltpu.make_async_copy(vpool.at[page], vbuf.at[slot, g], sem.at[slot]))
            out.append(pltpu.make_async_copy(lpool.at[page], lbuf.at[slot, g], sem.at[slot]))
        return out

    def start(bb, c, slot):
        for cp in copies(bb, c, slot):
            cp.start()

    def wait(bb, c, slot):
        for cp in copies(bb, c, slot):
            cp.wait()

    @pl.when(b == 0)
    def _():
        start(0, 0, 0)

    qbd = qbd_ref[0]
    cn = cn_ref[0][:, 0:1]
    tri = tri_ref[...]

    def step(c, slot, carry):
        m_old, l_old, acc, run = carry

        @pl.when(c + 1 < n_chunks)
        def _():
            start(b, c + 1, 1 - slot)

        @pl.when((c + 1 == n_chunks) & (b + 1 < nb))
        def _():
            start(b + 1, 0, 1 - slot)

        wait(b, c, slot)
        kc = kbuf[slot].reshape(ck, C_KV_W).astype(BF16)
        vc = vbuf[slot].reshape(ck, C_KV_W).astype(BF16)
        s = _dot_nt(qbd, kc)
        hi, mid, lo = _split3(lbuf[slot].reshape(chunk * C_HEADS, PAGE_SIZE))
        suf = _dot(hi, tri) + _dot(mid, tri) + _dot(lo, tri)
        parts = []
        for g in range(chunk):
            blk = suf[g * C_HEADS:(g + 1) * C_HEADS]
            parts.append(blk[:, :PAGE_SIZE] + run)
            run = run + blk[:, PAGE_SIZE:]
        bias = jnp.concatenate(parts, axis=-1)
        s = s + jnp.concatenate([bias] * t, axis=0) + cn
        m_new = jnp.maximum(m_old, jnp.max(s, axis=-1, keepdims=True))
        a = jnp.exp(m_old - m_new)
        p = jnp.exp(s - m_new)
        l_new = a * l_old + jnp.sum(p, axis=-1, keepdims=True)
        acc = a * acc + _dot(p.astype(BF16), vc)
        return m_new, l_new, acc, run

    def pair(j, carry):
        carry = step(2 * j, 0, carry)
        return step(2 * j + 1, 1, carry)

    init = (jnp.full((rows, 1), NEG_INF, F32), jnp.zeros((rows, 1), F32),
            jnp.zeros((rows, C_KV_W), F32), jnp.zeros((C_HEADS, PAGE_SIZE), F32))
    m_old, l_old, acc, _ = lax.fori_loop(0, n_chunks // 2, pair, init)

    s = _dot_nt(qbd, kn_ref[0]) + cn - cnk_ref[0]
    ti = lax.broadcasted_iota(jnp.int32, (rows, LANES), 0) // C_HEADS
    sj = lax.broadcasted_iota(jnp.int32, (rows, LANES), 1)
    s = jnp.where(sj <= ti, s, NEG_INF)
    m_new = jnp.maximum(m_old, jnp.max(s, axis=-1, keepdims=True))
    a = jnp.exp(m_old - m_new)
    p = jnp.exp(s - m_new)
    l_new = a * l_old + jnp.sum(p, axis=-1, keepdims=True)
    acc = (a * acc + _dot(p.astype(BF16), vn_ref[0])) / l_new
    kv_of_row = (lax.broadcasted_iota(jnp.int32, (rows, C_HEAD_DIM), 0) % C_HEADS) // C_GROUP
    out = jnp.zeros((rows, C_HEAD_DIM), F32)
    for kvh in range(C_KV_HEADS):
        out = jnp.where(kv_of_row == kvh, acc[:, kvh * C_HEAD_DIM:(kvh + 1) * C_HEAD_DIM], out)
    o_ref[0] = out


def _fox_decode(page_table, qbd, cn_col, cn_key, k_new, v_new, kpool, vpool, lpool_t):
    db, n_pages = page_table.shape
    t = qbd.shape[1] // C_HEADS
    rows = t * C_HEADS
    chunk = DECODE_PAGES_PER_CHUNK
    j = np.arange(PAGE_SIZE)
    tri = np.concatenate([(j[:, None] > j[None, :]).astype(np.float32),
                          np.ones((PAGE_SIZE, PAGE_SIZE), np.float32)], axis=1)
    tri = jnp.asarray(tri, BF16)
    per_b = lambda shape: pl.BlockSpec((1,) + shape, lambda b, pt: (b, 0, 0))
    hbm = pl.BlockSpec(memory_space=pl.ANY)
    grid_spec = pltpu.PrefetchScalarGridSpec(
        num_scalar_prefetch=1,
        grid=(db,),
        in_specs=[per_b((rows, C_KV_W)), per_b((rows, LANES)), per_b((rows, LANES)),
                  per_b((LANES, C_KV_W)), per_b((LANES, C_KV_W)),
                  pl.BlockSpec((PAGE_SIZE, 2 * PAGE_SIZE), lambda b, pt: (0, 0)),
                  hbm, hbm, hbm],
        out_specs=per_b((rows, C_HEAD_DIM)),
        scratch_shapes=[pltpu.VMEM((2, chunk, PAGE_SIZE, C_KV_W), F32),
                        pltpu.VMEM((2, chunk, PAGE_SIZE, C_KV_W), F32),
                        pltpu.VMEM((2, chunk, C_HEADS, PAGE_SIZE), F32),
                        pltpu.SemaphoreType.DMA((2,))])
    return pl.pallas_call(
        functools.partial(_fox_decode_kernel, n_pages=n_pages, chunk=chunk, t=t),
        grid_spec=grid_spec,
        out_shape=jax.ShapeDtypeStruct((db, rows, C_HEAD_DIM), F32),
        compiler_params=_params("arbitrary"),
        name="fox_decode",
    )(page_table, qbd, cn_col, cn_key, k_new, v_new, tri, kpool, vpool, lpool_t)


_ROUT_GROUP_LANE = 8


def _router_kernel(x_ref, whi_ref, wlo_ref, b_ref, o_ref):
    x = x_ref[...]
    xh = x.astype(BF16)
    xl = (x - xh.astype(F32)).astype(BF16)
    logit = _dot(xh, whi_ref[...]) + _dot(xl, whi_ref[...]) + _dot(xh, wlo_ref[...]) + b_ref[...]
    lane_i = lax.broadcasted_iota(jnp.int32, logit.shape, 1)
    lane = lane_i.astype(F32)
    e_lane = lane_i - N_GROUPS
    lane_group = lax.shift_right_arithmetic(e_lane, 2).astype(F32)
    lane_slot = jnp.bitwise_and(e_lane, EXPERTS_PER_GROUP - 1).astype(F32)
    is_e = (e_lane >= 0) & (e_lane < N_EXPERTS)

    def lane_max(mask):
        return jnp.max(jnp.where(mask, logit, -jnp.inf), axis=-1, keepdims=True)

    def first_lane(mask):
        return jnp.min(jnp.where(mask, lane, float(LANES)), axis=-1, keepdims=True)

    is_g = lane_i < N_GROUPS
    g_max = lane_max(is_g)
    g_idx = first_lane(is_g & (logit == g_max))
    g_sel = 1.0 / jnp.sum(jnp.where(is_g, jnp.exp(logit - g_max), 0.0), axis=-1, keepdims=True)
    in_g = is_e & (lane_group == g_idx)
    v1 = lane_max(in_g)
    i1 = first_lane(in_g & (logit == v1))
    rest = in_g & (lane != i1)
    v2 = lane_max(rest)
    i2 = first_lane(rest & (logit == v2))
    e2 = jnp.exp(v2 - v1)
    w1 = g_sel / (1.0 + e2)
    w2 = g_sel * e2 / (1.0 + e2)
    j1 = jnp.sum(jnp.where(lane == i1, lane_slot, 0.0), axis=-1, keepdims=True)
    j2 = jnp.sum(jnp.where(lane == i2, lane_slot, 0.0), axis=-1, keepdims=True)
    out = jnp.where(lane == j1, w1, 0.0) + jnp.where(lane == j2, w2, 0.0)
    o_ref[...] = jnp.where(lane_i == _ROUT_GROUP_LANE, g_idx, out)


def _router(x, w_rg, b_rg, w_re, b_re):
    n, d = x.shape
    bm = TOKEN_BLOCK
    w = jnp.concatenate([w_rg, w_re, jnp.zeros((d, LANES - N_GROUPS - N_EXPERTS), F32)], axis=1)
    whi = w.astype(BF16)
    wlo = (w - whi.astype(F32)).astype(BF16)
    bias = jnp.concatenate([b_rg, b_re, jnp.zeros((LANES - N_GROUPS - N_EXPERTS,), F32)]).reshape(1, LANES)
    return pl.pallas_call(
        _router_kernel,
        grid=(n // bm,),
        in_specs=[pl.BlockSpec((bm, d), lambda i: (i, 0)),
                  pl.BlockSpec((d, LANES), lambda i: (0, 0)),
                  pl.BlockSpec((d, LANES), lambda i: (0, 0)),
                  pl.BlockSpec((1, LANES), lambda i: (0, 0))],
        out_specs=pl.BlockSpec((bm, LANES), lambda i: (i, 0)),
        out_shape=jax.ShapeDtypeStruct((n, LANES), F32),
        compiler_params=_params("arbitrary"),
        name="router",
    )(x, whi, wlo, bias)


def _moe_kernel(tg_ref, nu_ref, src_ref, x_hbm, comb_ref, wg_ref, wu_ref, wd_ref, g_ref, b_ref,
                out_hbm, xbuf, obuf, gsem, ssem, *, tm):
    i = pl.program_id(0)
    n_used = nu_ref[0]

    def gather(tile, slot, wait):
        def body(r, carry):
            tok = jnp.maximum(src_ref[tile * tm + r], 0)
            cp = pltpu.make_async_copy(x_hbm.at[pl.ds(tok, 1)], xbuf.at[slot, pl.ds(r, 1)], gsem.at[slot])
            cp.wait() if wait else cp.start()
            return carry
        lax.fori_loop(0, tm, body, 0)

    def scatter(tile, wait):
        def body(r, carry):
            tok = src_ref[tile * tm + r]

            @pl.when(tok >= 0)
            def _():
                cp = pltpu.make_async_copy(obuf.at[pl.ds(r, 1)], out_hbm.at[pl.ds(tok, 1)], ssem.at[0])
                cp.wait() if wait else cp.start()
            return carry
        lax.fori_loop(0, tm, body, 0)

    @pl.when(i == 0)
    def _():
        gather(0, 0, False)

    @pl.when(i < n_used)
    def _():
        slot = i % 2

        @pl.when(i + 1 < n_used)
        def _():
            gather(i + 1, 1 - slot, False)

        gather(i, slot, True)
        x = xbuf[slot]
        xb = x.astype(BF16)
        comb = comb_ref[...]
        y = jnp.zeros(x.shape, F32)
        for e in range(EXPERTS_PER_GROUP):
            gate = _dot(xb, wg_ref[0, e])
            up = _dot(xb, wu_ref[0, e])
            h = gate * _sigmoid(gate) * up * comb[:, e:e + 1]
            y = y + _dot(h.astype(BF16), wd_ref[0, e])
        res = _layer_norm(DEEPNORM_ALPHA * x + y, g_ref[...], b_ref[...])

        @pl.when(i > 0)
        def _():
            scatter(i - 1, True)

        obuf[...] = res
        scatter(i, False)

        @pl.when(i == n_used - 1)
        def _():
            scatter(i, True)


def _moe(x, rout, w_gate, w_up, w_down, g, b):
    n, d = x.shape
    tm = MOE_TILE
    n_tiles = n // tm + N_GROUPS
    gidx = rout[:, _ROUT_GROUP_LANE].astype(jnp.int32)
    onehot = (gidx[:, None] == jnp.arange(N_GROUPS, dtype=jnp.int32)[None, :]).astype(jnp.int32)
    counts = jnp.sum(onehot, axis=0)
    rank = jnp.sum((jnp.cumsum(onehot, axis=0) - onehot) * onehot, axis=1)
    tile_off = jnp.concatenate([jnp.zeros((1,), jnp.int32), jnp.cumsum((counts + tm - 1) // tm)]).astype(jnp.int32)
    pos = tile_off[gidx] * tm + rank
    src = jnp.full((n_tiles * tm,), -1, jnp.int32).at[pos].set(jnp.arange(n, dtype=jnp.int32))
    n_used = tile_off[N_GROUPS:]
    tiles = jnp.minimum(jnp.arange(n_tiles, dtype=jnp.int32), n_used[0] - 1)
    tile_group = jnp.sum((tiles[:, None] >= tile_off[None, 1:N_GROUPS]).astype(jnp.int32), axis=1)
    comb = jnp.where(src[:, None] >= 0, rout[jnp.maximum(src, 0), :EXPERTS_PER_GROUP], 0.0)

    shape4 = (N_GROUPS, EXPERTS_PER_GROUP)
    wg = w_gate.astype(BF16).reshape(shape4 + w_gate.shape[1:])
    wu = w_up.astype(BF16).reshape(shape4 + w_up.shape[1:])
    wd = w_down.astype(BF16).reshape(shape4 + w_down.shape[1:])
    wmap = lambda i, tg, nu, src: (tg[i], 0, 0, 0)
    full = lambda i, tg, nu, src: (0, 0)
    hbm = pl.BlockSpec(memory_space=pl.ANY)
    grid_spec = pltpu.PrefetchScalarGridSpec(
        num_scalar_prefetch=3,
        grid=(n_tiles,),
        in_specs=[hbm,
                  pl.BlockSpec((tm, EXPERTS_PER_GROUP), lambda i, tg, nu, src: (i, 0)),
                  pl.BlockSpec((1, EXPERTS_PER_GROUP, d, D_EXPERT), wmap),
                  pl.BlockSpec((1, EXPERTS_PER_GROUP, d, D_EXPERT), wmap),
                  pl.BlockSpec((1, EXPERTS_PER_GROUP, D_EXPERT, d), wmap),
                  pl.BlockSpec((1, d), full),
                  pl.BlockSpec((1, d), full)],
        out_specs=hbm,
        scratch_shapes=[pltpu.VMEM((2, tm, d), F32),
                        pltpu.VMEM((tm, d), F32),
                        pltpu.SemaphoreType.DMA((2,)),
                        pltpu.SemaphoreType.DMA((1,))])
    return pl.pallas_call(
        functools.partial(_moe_kernel, tm=tm),
        grid_spec=grid_spec,
        out_shape=jax.ShapeDtypeStruct((n, d), F32),
        compiler_params=_params("arbitrary"),
        name="moe",
    )(tile_group, n_used, src, x, comb, wg, wu, wd, g.reshape(1, d), b.reshape(1, d))


def _mem_and_moe(xp, xs, mem2, cache_mem_k, cache_mem_v, layer, bsz, db, t_s, p):
    d = D_MODEL
    n_mem = mem2.shape[0] // bsz
    wq, wo = p['w_xq'][layer].astype(BF16), p['w_xo'][layer].astype(BF16)
    g1, b1 = p['ln_g'][layer, 1], p['ln_b'][layer, 1]
    wkv = jnp.concatenate([p['w_xk'][layer], p['w_xv'][layer]], axis=1).astype(BF16)
    mkv = _matmul(mem2, wkv)
    mk, mv = mkv[:, :d], mkv[:, d:]
    xp = _xattn_prompt(xp, mk, mv, wq, wo, g1, b1, bsz)
    qs = _matmul(xs, wq).reshape(db, t_s, d)
    os_ = _xattn_sample(qs, cache_mem_k[layer].reshape(db, -1, d), cache_mem_v[layer].reshape(db, -1, d))
    xs = _proj_norm([os_.reshape(db * t_s, d)], [wo], xs, g1, b1)
    x = jnp.concatenate([xp, xs], axis=0)
    rout = _router(x, p['w_rg'][layer], p['b_rg'][layer], p['w_re'][layer], p['b_re'][layer])
    x = _moe(x, rout, p['w_gate'][layer], p['w_up'][layer], p['w_down'][layer],
             p['ln_g'][layer, 2], p['ln_b'][layer, 2])
    n_p = xp.shape[0]
    return (x[:n_p], x[n_p:],
            mk.reshape(bsz, n_mem, MEM_HEADS, MEM_HEAD_DIM), mv.reshape(bsz, n_mem, MEM_HEADS, MEM_HEAD_DIM))


def kernel(x_prompt, x_sample, mem_prompt, cache_swa_k, cache_swa_v, state_ret, cache_fox_k, cache_fox_v, cache_fox_logf, cache_mem_k, cache_mem_v, page_table, w_in_ab, sink_ab, gn_gain_ab, w_out_ab, w_in_fox, b_forget, w_out_fox, w_xq, w_xk, w_xv, w_xo, ln_g, ln_b, w_rg, b_rg, w_re, b_re, w_gate, w_up, w_down):
    p = dict(w_xq=w_xq, w_xk=w_xk, w_xv=w_xv, w_xo=w_xo, ln_g=ln_g, ln_b=ln_b, w_rg=w_rg, b_rg=b_rg,
             w_re=w_re, b_re=b_re, w_gate=w_gate, w_up=w_up, w_down=w_down)
    bsz, seq, d = x_prompt.shape
    db, t_s, _ = x_sample.shape
    past_len = page_table.shape[1] * PAGE_SIZE
    xp = x_prompt.reshape(bsz * seq, d)
    xs = x_sample.reshape(db * t_s, d)
    mem2 = mem_prompt.reshape(-1, d)
    out = {}

    w_in = _permute_w_in_ab(w_in_ab[0]).astype(BF16)
    w_out = w_out_ab[0].astype(BF16)
    g0, b0 = ln_g[0, 0], ln_b[0, 0]
    zp = _matmul(xp, w_in)
    a_o = _swa_prompt(zp, sink_ab[0], seq)
    cos_p, sin_p = _rope_tables(jnp.arange(seq, dtype=jnp.int32))
    b_y, ret_p = _ret_prompt(zp.reshape(bsz, seq, _AB_W), cos_p, sin_p, gn_gain_ab[0])
    zp3 = zp.reshape(bsz, seq, _AB_W)
    out['swa_kp'] = zp3[:, seq - WINDOW:, _AB_AK:_AB_AK + A_KV_W].reshape(1, bsz, WINDOW, A_KV_HEADS, A_HEAD_DIM)
    out['swa_vp'] = zp3[:, seq - WINDOW:, _AB_AV:_AB_AV + A_KV_W].reshape(1, bsz, WINDOW, A_KV_HEADS, A_HEAD_DIM)
    xp = _proj_norm([a_o, b_y.reshape(bsz * seq, B_W)], [w_out[:A_Q_W], w_out[A_Q_W:]], xp, g0, b0)

    zs = _matmul(xs, w_in).reshape(db, t_s, _AB_W)
    cos_s, sin_s = _rope_tables(past_len + jnp.arange(t_s, dtype=jnp.int32))
    h_s, kb, vb, ret_s = _ab_sample(zs, cache_swa_k[0].reshape(db, WINDOW, A_KV_W),
                                    cache_swa_v[0].reshape(db, WINDOW, A_KV_W), state_ret[0],
                                    sink_ab[0], cos_s, sin_s, gn_gain_ab[0])
    xs = _proj_norm([h_s.reshape(db * t_s, A_Q_W + B_W)], [w_out], xs, g0, b0)
    xp, xs, mk0, mv0 = _mem_and_moe(xp, xs, mem2, cache_mem_k, cache_mem_v, 0, bsz, db, t_s, p)

    w_in = jnp.concatenate([w_in_fox[0], jnp.zeros((d, LANES - C_HEADS), F32)], axis=1).astype(BF16)
    w_out = w_out_fox[0].astype(BF16)
    g0, b0 = ln_g[1, 0], ln_b[1, 0]
    kc0, vc0 = C_Q_W, C_Q_W + C_KV_W
    zf = _matmul(xp, w_in)
    lf_p, c_col, c_row = _logf_cumsum(zf, b_forget[0], seq)
    c_col4 = c_col.reshape(bsz, seq, C_KV_HEADS, C_GROUP).transpose(0, 2, 1, 3)
    c_row4 = c_row.reshape(C_KV_HEADS, C_GROUP, bsz, seq).transpose(2, 0, 1, 3)
    o_p = _fox_prompt(zf, c_col4, c_row4, bsz, seq)
    xp = _proj_norm([o_p], [w_out], xp, g0, b0)
    fox_kp = zf[:, kc0:kc0 + C_KV_W].reshape(1, bsz, seq, C_KV_HEADS, C_HEAD_DIM)
    fox_vp = zf[:, vc0:vc0 + C_KV_W].reshape(1, bsz, seq, C_KV_HEADS, C_HEAD_DIM)

    zfs = _matmul(xs, w_in)
    lf_s, cn, _ = _logf_cumsum(zfs, b_forget[0], t_s)
    rows = t_s * C_HEADS
    q = zfs[:, :C_Q_W].reshape(db, t_s, C_HEADS, 1, C_HEAD_DIM) * (C_HEAD_DIM ** -0.5)
    own_kv = (jnp.arange(C_HEADS)[:, None] // C_GROUP == jnp.arange(C_KV_HEADS)[None, :])
    qbd = jnp.where(own_kv[None, None, :, :, None], q, 0.0).reshape(db, rows, C_KV_W).astype(BF16)
    cn3 = cn.reshape(db, t_s, C_HEADS)
    cn_col = jnp.broadcast_to(cn3.reshape(db, rows, 1), (db, rows, LANES))
    cn_key = jnp.broadcast_to(cn3.transpose(0, 2, 1)[:, None], (db, t_s, C_HEADS, t_s)).reshape(db, rows, t_s)
    cn_key = jnp.concatenate([cn_key, jnp.zeros((db, rows, LANES - t_s), F32)], axis=-1)
    k_new = zfs[:, kc0:kc0 + C_KV_W].reshape(db, t_s, C_KV_W)
    v_new = zfs[:, vc0:vc0 + C_KV_W].reshape(db, t_s, C_KV_W)
    zpad = jnp.zeros((db, LANES - t_s, C_KV_W), F32)
    o_s = _fox_decode(page_table, qbd, cn_col, cn_key,
                      jnp.concatenate([k_new, zpad], axis=1).astype(BF16),
                      jnp.concatenate([v_new, zpad], axis=1).astype(BF16),
                      cache_fox_k[0].reshape(-1, PAGE_SIZE, C_KV_W),
                      cache_fox_v[0].reshape(-1, PAGE_SIZE, C_KV_W),
                      cache_fox_logf[0].transpose(0, 2, 1))
    xs = _proj_norm([o_s.reshape(db * t_s, C_Q_W)], [w_out], xs, g0, b0)
    xp, xs, mk1, mv1 = _mem_and_moe(xp, xs, mem2, cache_mem_k, cache_mem_v, 1, bsz, db, t_s, p)

    kv5 = (1, db, t_s, C_KV_HEADS, C_HEAD_DIM)
    return (xp.reshape(bsz, seq, d), xs.reshape(db, t_s, d),
            out['swa_kp'], out['swa_vp'],
            kb.reshape(1, db, WINDOW, A_KV_HEADS, A_HEAD_DIM), vb.reshape(1, db, WINDOW, A_KV_HEADS, A_HEAD_DIM),
            ret_p[None], ret_s[None],
            fox_kp, fox_vp, lf_p.reshape(1, bsz, seq, C_HEADS),
            k_new.reshape(kv5), v_new.reshape(kv5), lf_s.reshape(1, db, t_s, C_HEADS),
            jnp.stack([mk0, mk1]), jnp.stack([mv0, mv1]))
```

```python
import functools

import numpy as np
import jax
import jax.numpy as jnp
from jax import lax
from jax.experimental import pallas as pl
from jax.experimental.pallas import tpu as pltpu

F32 = jnp.float32
BF16 = jnp.bfloat16

D_MODEL = 1024
DEPTH = 2
PAGE_SIZE = 128
A_HEADS = 8
A_KV_HEADS = 2
A_GROUP = A_HEADS // A_KV_HEADS
A_HEAD_DIM = 64
WINDOW = 128
B_HEADS = 4
B_KEY_DIM = 128
B_VAL_DIM = 128
RET_CHUNK = 128
ROPE_BASE = 10000.0
C_HEADS = 16
C_KV_HEADS = 4
C_GROUP = C_HEADS // C_KV_HEADS
C_HEAD_DIM = 64
MEM_HEADS = 4
MEM_HEAD_DIM = D_MODEL // MEM_HEADS
N_GROUPS = 4
EXPERTS_PER_GROUP = 4
N_EXPERTS = N_GROUPS * EXPERTS_PER_GROUP
D_EXPERT = D_MODEL // 2
LN_EPS = 1e-5
GN_EPS = 1e-6
DEEPNORM_ALPHA = (2 * DEPTH) ** 0.25
NEG_INF = -1e30

A_Q_W = A_HEADS * A_HEAD_DIM
A_KV_W = A_KV_HEADS * A_HEAD_DIM
B_W = B_HEADS * B_KEY_DIM
C_Q_W = C_HEADS * C_HEAD_DIM
C_KV_W = C_KV_HEADS * C_HEAD_DIM
IN_FOX_PAD_W = C_Q_W + 2 * C_KV_W + 128

LANES = 128
TOKEN_BLOCK = 512
MOE_TILE = 256
DECODE_PAGES_PER_CHUNK = 16
FOX_ROW_BLOCK = 256
VMEM_LIMIT = 56 * 1024 * 1024
LOG2E = 1.4426950408889634

_NT = (((1,), (1,)), ((), ()))
_TN = (((0,), (0,)), ((), ()))


def _params(*sem):
    return pltpu.CompilerParams(dimension_semantics=sem, vmem_limit_bytes=VMEM_LIMIT)


def _dot(a, b):
    return jnp.dot(a, b, preferred_element_type=F32)


def _dot_nt(a, b):
    return lax.dot_general(a, b, _NT, preferred_element_type=F32)


def _split3(x):
    hi = x.astype(BF16)
    r1 = x - hi.astype(F32)
    mid = r1.astype(BF16)
    lo = (r1 - mid.astype(F32)).astype(BF16)
    return hi, mid, lo


def _layer_norm(y, g, b):
    mu = jnp.mean(y, axis=-1, keepdims=True)
    d = y - mu
    var = jnp.mean(d * d, axis=-1, keepdims=True)
    return d * lax.rsqrt(var + LN_EPS) * g + b


def _sigmoid(x):
    return 1.0 / (1.0 + jnp.exp(-x))


def _mm_kernel(x_ref, w_ref, o_ref):
    o_ref[...] = _dot(x_ref[...].astype(BF16), w_ref[...]).astype(o_ref.dtype)


def _matmul(x, w, out_dtype=F32):
    m, k = x.shape
    n = w.shape[1]
    bm = min(TOKEN_BLOCK, m)
    return pl.pallas_call(
        _mm_kernel,
        grid=(m // bm,),
        in_specs=[pl.BlockSpec((bm, k), lambda i: (i, 0)),
                  pl.BlockSpec((k, n), lambda i: (0, 0))],
        out_specs=pl.BlockSpec((bm, n), lambda i: (i, 0)),
        out_shape=jax.ShapeDtypeStruct((m, n), out_dtype),
        compiler_params=_params("arbitrary"),
        name="matmul",
    )(x, w)


def _proj_norm_kernel(*refs, n_in):
    h_refs, w_refs = refs[:n_in], refs[n_in:2 * n_in]
    x_ref, g_ref, b_ref, o_ref = refs[2 * n_in:]
    acc = _dot(h_refs[0][...].astype(BF16), w_refs[0][...])
    for h_ref, w_ref in zip(h_refs[1:], w_refs[1:]):
        acc = acc + _dot(h_ref[...].astype(BF16), w_ref[...])
    o_ref[...] = _layer_norm(DEEPNORM_ALPHA * x_ref[...] + acc, g_ref[...], b_ref[...])


def _proj_norm(hs, ws, x, g, b):
    m, d = x.shape
    bm = min(TOKEN_BLOCK, m)
    n_in = len(hs)
    in_specs = ([pl.BlockSpec((bm, h.shape[1]), lambda i: (i, 0)) for h in hs]
                + [pl.BlockSpec(w.shape, lambda i: (0, 0)) for w in ws]
                + [pl.BlockSpec((bm, d), lambda i: (i, 0)),
                   pl.BlockSpec((1, d), lambda i: (0, 0)),
                   pl.BlockSpec((1, d), lambda i: (0, 0))])
    return pl.pallas_call(
        functools.partial(_proj_norm_kernel, n_in=n_in),
        grid=(m // bm,),
        in_specs=in_specs,
        out_specs=pl.BlockSpec((bm, d), lambda i: (i, 0)),
        out_shape=jax.ShapeDtypeStruct((m, d), F32),
        compiler_params=_params("arbitrary"),
        name="proj_norm",
    )(*hs, *ws, x, g.reshape(1, d), b.reshape(1, d))


_AB_Q, _AB_BQ, _AB_BK, _AB_BV, _AB_BG = 0, 512, 1024, 1536, 2048
_AB_AK, _AB_AV = 2560, 2688
_AB_W = 2816


def _permute_w_in_ab(w):
    aq, ak, av, bq, bk, bv, bg = jnp.split(
        w, np.cumsum([A_Q_W, A_KV_W, A_KV_W, B_W, B_W, B_W]).tolist(), axis=-1)
    return jnp.concatenate([aq, bq, bk, bv, bg, ak, av], axis=-1)


def _sink_softmax_pv(s, valid, sk, v):
    s = jnp.where(valid, s, NEG_INF)
    m = jnp.maximum(jnp.max(s, axis=-1, keepdims=True), sk)
    p = jnp.exp(s - m)
    den = jnp.sum(p, axis=-1, keepdims=True) + jnp.exp(sk - m)
    return _dot(p.astype(BF16), v) / den


def _sink_column(sink_ref, kvh, rows_per_head):
    n = A_GROUP * rows_per_head
    grp = lax.broadcasted_iota(jnp.int32, (n, 1), 0) // rows_per_head
    sk = jnp.full((n, 1), sink_ref[kvh * A_GROUP], F32)
    for g in range(1, A_GROUP):
        sk = jnp.where(grp == g, sink_ref[kvh * A_GROUP + g], sk)
    return sk


def _swa_prompt_kernel(sink_ref, q_ref, kc_ref, vc_ref, kp_ref, vp_ref, o_ref, *, tq, blocks_per_seq):
    i = pl.program_id(0)
    lo = jnp.where(i % blocks_per_seq == 0, WINDOW, 0)
    kk = jnp.concatenate([kp_ref[...], kc_ref[...]], axis=0).astype(BF16)
    vv = jnp.concatenate([vp_ref[...], vc_ref[...]], axis=0).astype(BF16)
    n = A_GROUP * WINDOW
    qi = lax.broadcasted_iota(jnp.int32, (n, 2 * WINDOW), 0) % WINDOW
    kj = lax.broadcasted_iota(jnp.int32, (n, 2 * WINDOW), 1)
    band = (kj >= qi) & (kj <= qi + WINDOW)
    band0 = band & (kj >= lo)
    for j in range(tq // WINDOW):
        q = q_ref[j * WINDOW:(j + 1) * WINDOW, :]
        valid = band0 if j == 0 else band
        for kvh in range(A_KV_HEADS):
            heads = [q[:, (kvh * A_GROUP + g) * A_HEAD_DIM:(kvh * A_GROUP + g + 1) * A_HEAD_DIM]
                     for g in range(A_GROUP)]
            q4 = (jnp.concatenate(heads, axis=0) * (A_HEAD_DIM ** -0.5)).astype(BF16)
            ksl = kk[j * WINDOW:(j + 2) * WINDOW, kvh * A_HEAD_DIM:(kvh + 1) * A_HEAD_DIM]
            vsl = vv[j * WINDOW:(j + 2) * WINDOW, kvh * A_HEAD_DIM:(kvh + 1) * A_HEAD_DIM]
            o = _sink_softmax_pv(_dot_nt(q4, ksl), valid, _sink_column(sink_ref, kvh, WINDOW), vsl)
            for g in range(A_GROUP):
                c0 = (kvh * A_GROUP + g) * A_HEAD_DIM
                o_ref[j * WINDOW:(j + 1) * WINDOW, c0:c0 + A_HEAD_DIM] = o[g * WINDOW:(g + 1) * WINDOW]


def _swa_prompt(z, sink, seq):
    n = z.shape[0]
    tq = TOKEN_BLOCK
    sub = tq // WINDOW
    kcol, vcol = _AB_AK // LANES, _AB_AV // LANES
    return pl.pallas_call(
        functools.partial(_swa_prompt_kernel, tq=tq, blocks_per_seq=seq // tq),
        grid=(n // tq,),
        in_specs=[pl.BlockSpec(memory_space=pltpu.SMEM),
                  pl.BlockSpec((tq, A_Q_W), lambda i: (i, 0)),
                  pl.BlockSpec((tq, LANES), lambda i: (i, kcol)),
                  pl.BlockSpec((tq, LANES), lambda i: (i, vcol)),
                  pl.BlockSpec((WINDOW, LANES), lambda i: (jnp.maximum(i * sub - 1, 0), kcol)),
                  pl.BlockSpec((WINDOW, LANES), lambda i: (jnp.maximum(i * sub - 1, 0), vcol))],
        out_specs=pl.BlockSpec((tq, A_Q_W), lambda i: (i, 0)),
        out_shape=jax.ShapeDtypeStruct((n, A_Q_W), F32),
        compiler_params=_params("arbitrary"),
        name="swa_prompt",
    )(sink, z, z, z, z, z)


def _rope_tables(pos):
    half = B_KEY_DIM // 2
    inv_freq = 1.0 / (ROPE_BASE ** jnp.linspace(0.0, 1.0, half, dtype=F32))
    ang = pos.astype(F32)[:, None] * inv_freq[None, :]
    cos, sin = jnp.cos(ang), jnp.sin(ang)
    return jnp.concatenate([cos, cos], axis=-1), jnp.concatenate([-sin, sin], axis=-1)


def _rope(x, cos2, sin2):
    return x * cos2 + pltpu.roll(x, B_KEY_DIM // 2, 1) * sin2


def _ret_log_gamma():
    return jnp.log(1.0 - 2.0 ** (-5.0 - jnp.arange(B_HEADS, dtype=F32)))


def _group_norm_gate(o, gate, gain):
    mu = jnp.mean(o, axis=-1, keepdims=True)
    d = o - mu
    var = jnp.mean(d * d, axis=-1, keepdims=True)
    return gate * _sigmoid(gate) * (d * lax.rsqrt(var + GN_EPS) * gain)


def _ret_prompt_kernel(lg_ref, q_ref, k_ref, v_ref, g_ref, cos_ref, sin_ref, gain_ref,
                       y_ref, ret_ref, s_ref):
    h = pl.program_id(0)
    c = pl.program_id(1)
    n = RET_CHUNK

    @pl.when(c == 0)
    def _():
        s_ref[...] = jnp.zeros_like(s_ref)

    lg = lg_ref[h]
    r = lax.broadcasted_iota(jnp.int32, (n, n), 0)
    col = lax.broadcasted_iota(jnp.int32, (n, n), 1)
    diff = (r - col).astype(F32)
    decay = jnp.where(diff >= 0, jnp.exp(lg * jnp.maximum(diff, 0.0)), 0.0)
    rf = r.astype(F32)
    e_q = jnp.exp(lg * (rf + 1.0))
    e_k = jnp.exp(lg * (n - 1.0 - rf))
    e_c = jnp.exp(lg * jnp.full((1, n), float(n), F32))
    cos2, sin2 = cos_ref[...], sin_ref[...]
    gain = gain_ref[...]
    for b in range(q_ref.shape[0]):
        q = _rope(q_ref[b], cos2, sin2)
        k = _rope(k_ref[b], cos2, sin2) * (B_KEY_DIM ** -0.5)
        v = v_ref[b].astype(BF16)
        s0 = s_ref[b]
        qb = q.astype(BF16)
        inner = _dot_nt(qb, k.astype(BF16)) * decay
        o = _dot(inner.astype(BF16), v) + _dot(qb, s0.astype(BF16)) * e_q
        kw = (k * e_k).astype(BF16)
        s_ref[b] = s0 * e_c + lax.dot_general(kw, v, _TN, preferred_element_type=F32)
        y_ref[b] = _group_norm_gate(o, g_ref[b], gain)

    @pl.when(c == pl.num_programs(1) - 1)
    def _():
        for b in range(q_ref.shape[0]):
            ret_ref[b, 0] = s_ref[b]


def _ret_prompt(z3, cos2, sin2, gain):
    bsz, t, _ = z3.shape
    n = RET_CHUNK

    def col(base):
        return lambda h, c: (0, c, base // LANES + h)

    return pl.pallas_call(
        _ret_prompt_kernel,
        grid=(B_HEADS, t // n),
        in_specs=[pl.BlockSpec(memory_space=pltpu.SMEM),
                  pl.BlockSpec((bsz, n, LANES), col(_AB_BQ)),
                  pl.BlockSpec((bsz, n, LANES), col(_AB_BK)),
                  pl.BlockSpec((bsz, n, LANES), col(_AB_BV)),
                  pl.BlockSpec((bsz, n, LANES), col(_AB_BG)),
                  pl.BlockSpec((n, LANES), lambda h, c: (c, 0)),
                  pl.BlockSpec((n, LANES), lambda h, c: (c, 0)),
                  pl.BlockSpec((1, LANES), lambda h, c: (0, h))],
        out_specs=[pl.BlockSpec((bsz, n, LANES), lambda h, c: (0, c, h)),
                   pl.BlockSpec((bsz, 1, B_KEY_DIM, B_VAL_DIM), lambda h, c: (0, h, 0, 0))],
        out_shape=[jax.ShapeDtypeStruct((bsz, t, B_W), F32),
                   jax.ShapeDtypeStruct((bsz, B_HEADS, B_KEY_DIM, B_VAL_DIM), F32)],
        scratch_shapes=[pltpu.VMEM((bsz, B_KEY_DIM, B_VAL_DIM), F32)],
        compiler_params=_params("arbitrary", "arbitrary"),
        name="ret_prompt",
    )(_ret_log_gamma(), z3, z3, z3, z3, cos2, sin2, gain.reshape(1, B_W))


def _ab_sample_kernel(sink_ref, lg_ref, z_ref, ck_ref, cv_ref, st_ref, cos_ref, sin_ref, gain_ref,
                      h_ref, kb_ref, vb_ref, sn_ref, *, t):
    w = WINDOW
    n = A_GROUP * t
    pad = jnp.zeros((w - t, LANES), F32)
    ti = lax.broadcasted_iota(jnp.int32, (n, 2 * w), 0) % t
    kj = lax.broadcasted_iota(jnp.int32, (n, 2 * w), 1)
    valid = (kj >= ti) & (kj <= ti + w)
    cos2, sin2 = cos_ref[...], sin_ref[...]
    r = lax.broadcasted_iota(jnp.int32, (w, w), 0)
    col = lax.broadcasted_iota(jnp.int32, (w, w), 1)
    diff = (r - col).astype(F32)
    rf = r.astype(F32)
    live = (r < t) & (col < t)
    for b in range(z_ref.shape[0]):
        z = z_ref[b]
        k_new = z[:, _AB_AK:_AB_AK + A_KV_W]
        v_new = z[:, _AB_AV:_AB_AV + A_KV_W]
        kk = jnp.concatenate([ck_ref[b], k_new, pad], axis=0)
        vv = jnp.concatenate([cv_ref[b], v_new, pad], axis=0)
        kb_ref[b] = kk[t:t + w]
        vb_ref[b] = vv[t:t + w]
        kkb, vvb = kk.astype(BF16), vv.astype(BF16)
        a_parts = []
        for kvh in range(A_KV_HEADS):
            heads = [z[:, (kvh * A_GROUP + g) * A_HEAD_DIM:(kvh * A_GROUP + g + 1) * A_HEAD_DIM]
                     for g in range(A_GROUP)]
            q4 = (jnp.concatenate(heads, axis=0) * (A_HEAD_DIM ** -0.5)).astype(BF16)
            ksl = kkb[:, kvh * A_HEAD_DIM:(kvh + 1) * A_HEAD_DIM]
            vsl = vvb[:, kvh * A_HEAD_DIM:(kvh + 1) * A_HEAD_DIM]
            o = _sink_softmax_pv(_dot_nt(q4, ksl), valid, _sink_column(sink_ref, kvh, t), vsl)
            a_parts += [o[g * t:(g + 1) * t] for g in range(A_GROUP)]
        b_parts = []
        for hd in range(B_HEADS):
            lg = lg_ref[hd]
            decay = jnp.where(live & (diff >= 0), jnp.exp(lg * jnp.maximum(diff, 0.0)), 0.0)
            e_q = jnp.exp(lg * (rf + 1.0))
            e_k = jnp.where(r < t, jnp.exp(lg * (t - 1.0 - rf)), 0.0)
            e_c = jnp.exp(lg * jnp.full((1, w), float(t), F32))
            sl = slice(hd * B_KEY_DIM, (hd + 1) * B_KEY_DIM)
            zq = jnp.concatenate([z[:, _AB_BQ + hd * B_KEY_DIM:_AB_BQ + (hd + 1) * B_KEY_DIM], pad], axis=0)
            zk = jnp.concatenate([z[:, _AB_BK + hd * B_KEY_DIM:_AB_BK + (hd + 1) * B_KEY_DIM], pad], axis=0)
            zv = jnp.concatenate([z[:, _AB_BV + hd * B_VAL_DIM:_AB_BV + (hd + 1) * B_VAL_DIM], pad], axis=0)
            zg = z[:, _AB_BG + hd * B_VAL_DIM:_AB_BG + (hd + 1) * B_VAL_DIM]
            q = _rope(zq, cos2, sin2)
            k = _rope(zk, cos2, sin2) * (B_KEY_DIM ** -0.5)
            v = zv.astype(BF16)
            s0 = st_ref[b, hd]
            qb = q.astype(BF16)
            inner = _dot_nt(qb, k.astype(BF16)) * decay
            o = _dot(inner.astype(BF16), v) + _dot(qb, s0.astype(BF16)) * e_q
            kw = (k * e_k).astype(BF16)
            sn_ref[b, hd] = s0 * e_c + lax.dot_general(kw, v, _TN, preferred_element_type=F32)
            b_parts.append(_group_norm_gate(o[:t], zg, gain_ref[:, sl]))
        h_ref[b] = jnp.concatenate(a_parts + b_parts, axis=-1)


def _ab_sample(z3, cache_k, cache_v, state, sink, cos2, sin2, gain):
    db, t, _ = z3.shape
    bb = 4
    w = WINDOW
    cos_p = jnp.concatenate([cos2, jnp.ones((w - t, LANES), F32)], axis=0)
    sin_p = jnp.concatenate([sin2, jnp.zeros((w - t, LANES), F32)], axis=0)
    smem = pl.BlockSpec(memory_space=pltpu.SMEM)
    return pl.pallas_call(
        functools.partial(_ab_sample_kernel, t=t),
        grid=(db // bb,),
        in_specs=[smem, smem,
                  pl.BlockSpec((bb, t, _AB_W), lambda i: (i, 0, 0)),
                  pl.BlockSpec((bb, w, A_KV_W), lambda i: (i, 0, 0)),
                  pl.BlockSpec((bb, w, A_KV_W), lambda i: (i, 0, 0)),
                  pl.BlockSpec((bb, B_HEADS, B_KEY_DIM, B_VAL_DIM), lambda i: (i, 0, 0, 0)),
                  pl.BlockSpec((w, LANES), lambda i: (0, 0)),
                  pl.BlockSpec((w, LANES), lambda i: (0, 0)),
                  pl.BlockSpec((1, B_W), lambda i: (0, 0))],
        out_specs=[pl.BlockSpec((bb, t, A_Q_W + B_W), lambda i: (i, 0, 0)),
                   pl.BlockSpec((bb, w, A_KV_W), lambda i: (i, 0, 0)),
                   pl.BlockSpec((bb, w, A_KV_W), lambda i: (i, 0, 0)),
                   pl.BlockSpec((bb, B_HEADS, B_KEY_DIM, B_VAL_DIM), lambda i: (i, 0, 0, 0))],
        out_shape=[jax.ShapeDtypeStruct((db, t, A_Q_W + B_W), F32),
                   jax.ShapeDtypeStruct((db, w, A_KV_W), F32),
                   jax.ShapeDtypeStruct((db, w, A_KV_W), F32),
                   jax.ShapeDtypeStruct((db, B_HEADS, B_KEY_DIM, B_VAL_DIM), F32)],
        compiler_params=_params("arbitrary"),
        name="ab_sample",
    )(sink, _ret_log_gamma(), z3, cache_k, cache_v, state, cos_p, sin_p, gain.reshape(1, B_W))


def _softmax_pv(s, v):
    m = jnp.max(s, axis=-1, keepdims=True)
    p = jnp.exp(s - m)
    return _dot(p.astype(BF16), v) / jnp.sum(p, axis=-1, keepdims=True)


def _xattn_prompt_kernel(x_ref, wq_ref, wo_ref, mk_ref, mv_ref, g_ref, b_ref, o_ref):
    x = x_ref[...]
    q = _dot(x.astype(BF16), wq_ref[...])
    outs = []
    for h in range(MEM_HEADS):
        sl = slice(h * MEM_HEAD_DIM, (h + 1) * MEM_HEAD_DIM)
        qh = (q[:, sl] * (MEM_HEAD_DIM ** -0.5)).astype(BF16)
        s = _dot_nt(qh, mk_ref[:, sl].astype(BF16))
        outs.append(_softmax_pv(s, mv_ref[:, sl].astype(BF16)).astype(BF16))
    y = DEEPNORM_ALPHA * x + _dot(jnp.concatenate(outs, axis=-1), wo_ref[...])
    o_ref[...] = _layer_norm(y, g_ref[...], b_ref[...])


def _xattn_prompt(x, mk, mv, wq, wo, g, b, bsz):
    n, d = x.shape
    bm = TOKEN_BLOCK
    per = n // bsz // bm
    n_mem = mk.shape[0] // bsz
    full = lambda bi, i: (0, 0)
    return pl.pallas_call(
        _xattn_prompt_kernel,
        grid=(bsz, per),
        in_specs=[pl.BlockSpec((bm, d), lambda bi, i: (bi * per + i, 0)),
                  pl.BlockSpec((d, d), full),
                  pl.BlockSpec((d, d), full),
                  pl.BlockSpec((n_mem, d), lambda bi, i: (bi, 0)),
                  pl.BlockSpec((n_mem, d), lambda bi, i: (bi, 0)),
                  pl.BlockSpec((1, d), full),
                  pl.BlockSpec((1, d), full)],
        out_specs=pl.BlockSpec((bm, d), lambda bi, i: (bi * per + i, 0)),
        out_shape=jax.ShapeDtypeStruct((n, d), F32),
        compiler_params=_params("arbitrary", "arbitrary"),
        name="xattn_prompt",
    )(x, wq, wo, mk, mv, g.reshape(1, d), b.reshape(1, d))


def _xattn_sample_kernel(q_ref, k_ref, v_ref, o_ref, *, t):
    pad = jnp.zeros((8 - t, D_MODEL), F32)
    for b in range(q_ref.shape[0]):
        q = jnp.concatenate([q_ref[b], pad], axis=0)
        outs = []
        for h in range(MEM_HEADS):
            sl = slice(h * MEM_HEAD_DIM, (h + 1) * MEM_HEAD_DIM)
            qh = (q[:, sl] * (MEM_HEAD_DIM ** -0.5)).astype(BF16)
            s = _dot_nt(qh, k_ref[0, b, :, h, :].astype(BF16))
            outs.append(_softmax_pv(s, v_ref[0, b, :, h, :].astype(BF16)))
        o_ref[b] = jnp.concatenate(outs, axis=-1)[:t]


def _xattn_sample(q3, cache_k, cache_v, layer):
    db, t, d = q3.shape
    n_mem = cache_k.shape[2]
    bb = 4
    cache_spec = pl.BlockSpec((1, bb, n_mem, MEM_HEADS, MEM_HEAD_DIM), lambda i: (layer, i, 0, 0, 0))
    return pl.pallas_call(
        functools.partial(_xattn_sample_kernel, t=t),
        grid=(db // bb,),
        in_specs=[pl.BlockSpec((bb, t, d), lambda i: (i, 0, 0)), cache_spec, cache_spec],
        out_specs=pl.BlockSpec((bb, t, d), lambda i: (i, 0, 0)),
        out_shape=jax.ShapeDtypeStruct((db, t, d), F32),
        compiler_params=_params("arbitrary"),
        name="xattn_sample",
    )(q3, cache_k, cache_v)


def _log_sigmoid(x):
    return -(jnp.maximum(-x, 0.0) + jnp.log1p(jnp.exp(-jnp.abs(x))))


def _logf_kernel(f_ref, bf_ref, lf_ref, c_ref, *, seg):
    n = LANES
    lf = _log_sigmoid(f_ref[...] + bf_ref[...])
    r = lax.broadcasted_iota(jnp.int32, (n, n), 0)
    col = lax.broadcasted_iota(jnp.int32, (n, n), 1)
    tri = jnp.where((col <= r) & (col // seg == r // seg), 1.0, 0.0).astype(BF16)
    hi, mid, lo = _split3(lf)
    c = _dot(tri, hi) + _dot(tri, mid) + _dot(tri, lo)
    lf_ref[...] = lf[:, :C_HEADS]
    c_ref[...] = c[:, :C_HEADS]


def _pad_forget_bias(b_f):
    return jnp.concatenate([b_f.astype(F32), jnp.zeros((LANES - C_HEADS,), F32)]).reshape(1, LANES)


def _logf_cumsum(zf, b_f, seg):
    n = zf.shape[0]
    assert LANES % seg == 0
    fcol = (C_Q_W + 2 * C_KV_W) // LANES
    return pl.pallas_call(
        functools.partial(_logf_kernel, seg=seg),
        grid=(n // LANES,),
        in_specs=[pl.BlockSpec((LANES, LANES), lambda i: (i, fcol)),
                  pl.BlockSpec((1, LANES), lambda i: (0, 0))],
        out_specs=[pl.BlockSpec((LANES, C_HEADS), lambda i: (i, 0)),
                   pl.BlockSpec((LANES, C_HEADS), lambda i: (i, 0))],
        out_shape=[jax.ShapeDtypeStruct((n, C_HEADS), F32),
                   jax.ShapeDtypeStruct((n, C_HEADS), F32)],
        compiler_params=_params("arbitrary"),
        name="logf_cumsum",
    )(zf, _pad_forget_bias(b_f))


_BIAS_LANE = C_HEAD_DIM
_N_PIECES = 3


def _fox_feature_tables():
    pq = np.zeros((_N_PIECES * LANES, C_HEADS * LANES), np.float32)
    cq = np.zeros((1, C_HEADS * LANES), np.float32)
    pk = np.zeros((_N_PIECES * LANES, C_KV_HEADS * LANES), np.float32)
    ck = np.zeros((1, C_KV_HEADS * LANES), np.float32)
    cv = np.zeros((1, C_KV_HEADS * LANES), np.float32)
    for h in range(C_HEADS):
        kv, g = divmod(h, C_GROUP)
        own = _BIAS_LANE + _N_PIECES * (1 + g)
        for piece in range(_N_PIECES):
            pq[piece * LANES + h, h * LANES + _BIAS_LANE + piece] = 1.0
            pk[piece * LANES + h, kv * LANES + own + piece] = -1.0
        cq[0, h * LANES + own:h * LANES + own + _N_PIECES] = 1.0
    for kv in range(C_KV_HEADS):
        ck[0, kv * LANES + _BIAS_LANE:kv * LANES + _BIAS_LANE + _N_PIECES] = 1.0
        cv[0, kv * LANES + _BIAS_LANE] = 1.0
    return (jnp.asarray(pq, BF16), jnp.asarray(pk, BF16), jnp.asarray(cq), jnp.asarray(ck), jnp.asarray(cv))


def _slots(x, n_heads):
    z = jnp.zeros((x.shape[0], LANES - C_HEAD_DIM), F32)
    parts = []
    for h in range(n_heads):
        parts += [x[:, h * C_HEAD_DIM:(h + 1) * C_HEAD_DIM], z]
    return jnp.concatenate(parts, axis=-1)


def _fox_features_kernel(q_ref, kv_ref, f_ref, bf_ref, pq_ref, pk_ref, cq_ref, ck_ref, cv_ref,
                         lf_ref, qa_ref, ka_ref, va_ref, carry_ref, *, blocks_per_seq):
    i = pl.program_id(0)
    n = LANES
    lf = _log_sigmoid(f_ref[...] + bf_ref[...])
    r = lax.broadcasted_iota(jnp.int32, (n, n), 0)
    col = lax.broadcasted_iota(jnp.int32, (n, n), 1)
    tri = jnp.where(col <= r, 1.0, 0.0).astype(BF16)
    hi, mid, lo = _split3(lf)

    @pl.when(i % blocks_per_seq == 0)
    def _():
        carry_ref[...] = jnp.zeros_like(carry_ref)

    c = _dot(tri, hi) + _dot(tri, mid) + _dot(tri, lo) + carry_ref[0:1, :]
    carry_ref[...] = jnp.broadcast_to(c[n - 1:n, :], carry_ref.shape)
    lf_ref[...] = lf[:, :C_HEADS]
    pieces = jnp.concatenate(_split3(c * LOG2E), axis=-1)
    q = q_ref[...] * (C_HEAD_DIM ** -0.5 * LOG2E)
    qa_ref[...] = (_slots(q, C_HEADS) + _dot(pieces, pq_ref[...]) + cq_ref[...]).astype(BF16)
    kv = kv_ref[...]
    ka_ref[...] = (_slots(kv[:, :C_KV_W], C_KV_HEADS) + _dot(pieces, pk_ref[...]) + ck_ref[...]).astype(BF16)
    va_ref[...] = (_slots(kv[:, C_KV_W:], C_KV_HEADS) + cv_ref[...]).astype(BF16)


def _fox_features(zf, b_f, seq):
    n = zf.shape[0]
    pq, pk, cq, ck, cv = _fox_feature_tables()
    full = lambda i: (0, 0)
    qw, kw = C_HEADS * LANES, C_KV_HEADS * LANES
    return pl.pallas_call(
        functools.partial(_fox_features_kernel, blocks_per_seq=seq // LANES),
        grid=(n // LANES,),
        in_specs=[pl.BlockSpec((LANES, C_Q_W), lambda i: (i, 0)),
                  pl.BlockSpec((LANES, 2 * C_KV_W), lambda i: (i, C_Q_W // (2 * C_KV_W))),
                  pl.BlockSpec((LANES, LANES), lambda i: (i, (C_Q_W + 2 * C_KV_W) // LANES)),
                  pl.BlockSpec((1, LANES), full),
                  pl.BlockSpec(pq.shape, full), pl.BlockSpec(pk.shape, full),
                  pl.BlockSpec(cq.shape, full), pl.BlockSpec(ck.shape, full), pl.BlockSpec(cv.shape, full)],
        out_specs=[pl.BlockSpec((LANES, C_HEADS), lambda i: (i, 0)),
                   pl.BlockSpec((LANES, qw), lambda i: (i, 0)),
                   pl.BlockSpec((LANES, kw), lambda i: (i, 0)),
                   pl.BlockSpec((LANES, kw), lambda i: (i, 0))],
        out_shape=[jax.ShapeDtypeStruct((n, C_HEADS), F32),
                   jax.ShapeDtypeStruct((n, qw), BF16),
                   jax.ShapeDtypeStruct((n, kw), BF16),
                   jax.ShapeDtypeStruct((n, kw), BF16)],
        scratch_shapes=[pltpu.VMEM((8, LANES), F32)],
        compiler_params=_params("arbitrary"),
        name="fox_features",
    )(zf, zf, zf, _pad_forget_bias(b_f), pq, pk, cq, ck, cv)


def _fox_prompt_kernel(qi_ref, ki_ref, q_ref, k_ref, v_ref, o_ref, q4_s, m_s, acc_s, *, tq, rb):
    p = pl.program_id(1)
    qi = qi_ref[p]
    ki = ki_ref[p]
    n = C_GROUP * tq

    @pl.when(ki == 0)
    def _():
        for g in range(C_GROUP):
            q4_s[g * tq:(g + 1) * tq, :] = q_ref[:, g * LANES:(g + 1) * LANES]
        m_s[...] = jnp.full_like(m_s, NEG_INF)
        acc_s[...] = jnp.zeros_like(acc_s)

    k = k_ref[...]
    v = v_ref[...]

    def sweep(on_diagonal):
        for j in range(n // rb):
            rows = slice(j * rb, (j + 1) * rb)
            s = _dot_nt(q4_s[rows, :], k)
            if on_diagonal:
                q_pos = (j % (tq // rb)) * rb + lax.broadcasted_iota(jnp.int32, (rb, tq), 0)
                k_pos = lax.broadcasted_iota(jnp.int32, (rb, tq), 1)
                s = jnp.where(k_pos <= q_pos, s, NEG_INF)
            m_old = m_s[rows, :]
            m_new = jnp.maximum(m_old, jnp.max(s, axis=-1, keepdims=True))
            pr = jnp.exp2(s - jnp.concatenate([m_new] * (tq // LANES), axis=-1))
            acc_s[rows, :] = jnp.exp2(m_old - m_new) * acc_s[rows, :] + _dot(pr.astype(BF16), v)
            m_s[rows, :] = m_new

    @pl.when(ki < qi)
    def _():
        sweep(False)

    @pl.when(ki == qi)
    def _():
        sweep(True)
        acc = acc_s[...]
        o = acc[:, :C_HEAD_DIM] / acc[:, _BIAS_LANE:_BIAS_LANE + 1]
        o_ref[...] = jnp.concatenate([o[g * tq:(g + 1) * tq] for g in range(C_GROUP)], axis=-1)


def _fox_prompt(qa, ka, va, bsz, seq):
    n = qa.shape[0]
    tq = TOKEN_BLOCK
    nq = seq // tq
    pairs = [(q, k) for q in range(nq) for k in range(q + 1)]
    qi = jnp.asarray([p[0] for p in pairs], jnp.int32)
    ki = jnp.asarray([p[1] for p in pairs], jnp.int32)

    def qmap(bk, p, qi, ki):
        return ((bk // C_KV_HEADS) * nq + qi[p], bk % C_KV_HEADS)

    def kmap(bk, p, qi, ki):
        return ((bk // C_KV_HEADS) * nq + ki[p], bk % C_KV_HEADS)

    grid_spec = pltpu.PrefetchScalarGridSpec(
        num_scalar_prefetch=2,
        grid=(bsz * C_KV_HEADS, len(pairs)),
        in_specs=[pl.BlockSpec((tq, C_GROUP * LANES), qmap),
                  pl.BlockSpec((tq, LANES), kmap),
                  pl.BlockSpec((tq, LANES), kmap)],
        out_specs=pl.BlockSpec((tq, C_GROUP * C_HEAD_DIM), qmap),
        scratch_shapes=[pltpu.VMEM((C_GROUP * tq, LANES), BF16),
                        pltpu.VMEM((C_GROUP * tq, LANES), F32),
                        pltpu.VMEM((C_GROUP * tq, LANES), F32)])
    return pl.pallas_call(
        functools.partial(_fox_prompt_kernel, tq=tq, rb=FOX_ROW_BLOCK),
        grid_spec=grid_spec,
        out_shape=jax.ShapeDtypeStruct((n, C_Q_W), F32),
        compiler_params=_params("arbitrary", "arbitrary"),
        name="fox_prompt",
    )(qi, ki, qa, ka, va)


def _fox_decode_kernel(pt_ref, qbd_ref, cn_ref, cnk_ref, kn_ref, vn_ref, tri_ref,
                       kpool, vpool, lpool, o_ref, kbuf, vbuf, lbuf, sem, *, n_pages, chunk, t):
    b = pl.program_id(0)
    nb = pl.num_programs(0)
    n_chunks = n_pages // chunk
    rows = t * C_HEADS

    def copies(bb, c, slot):
        out = []
        for g in range(chunk):
            page = pt_ref[bb, n_pages - 1 - (c * chunk + g)]
            out.append(pltpu.make_async_copy(kpool.at[page], kbuf.at[slot, g], sem.at[slot]))
            out.append(pltpu.make_async_copy(vpool.at[page], vbuf.at[slot, g], sem.at[slot]))
            out.append(pltpu.make_async_copy(lpool.at[page], lbuf.at[slot, g], sem.at[slot]))
        return out

    def start(bb, c, slot):
        for cp in copies(bb, c, slot):
            cp.start()

    def wait(bb, c, slot):
        for cp in copies(bb, c, slot):
            cp.wait()

    @pl.when(b == 0)
    def _():
        start(0, 0, 0)

    qbd = qbd_ref[0]
    cn = cn_ref[0][:, 0:1]
    tri = tri_ref[...]

    def step(c, slot, carry):
        m_old, l_old, acc, run = carry

        @pl.when(c + 1 < n_chunks)
        def _():
            start(b, c + 1, 1 - slot)

        @pl.when((c + 1 == n_chunks) & (b + 1 < nb))
        def _():
            start(b + 1, 0, 1 - slot)

        wait(b, c, slot)
        kc = jnp.concatenate([kbuf[slot, g].astype(BF16) for g in range(chunk)], axis=-1)
        s = _dot(qbd, kc)
        hi, mid, lo = _split3(lbuf[slot].reshape(chunk * C_HEADS, PAGE_SIZE))
        suf = _dot(hi, tri) + _dot(mid, tri) + _dot(lo, tri)
        parts = []
        for g in range(chunk):
            blk = suf[g * C_HEADS:(g + 1) * C_HEADS]
            parts.append(blk[:, :PAGE_SIZE] + run)
            run = run + blk[:, PAGE_SIZE:]
        bias = jnp.concatenate(parts, axis=-1)
        s = s + jnp.concatenate([bias] * t, axis=0) + cn
        m_new = jnp.maximum(m_old, jnp.max(s, axis=-1, keepdims=True))
        a = jnp.exp(m_old - m_new)
        p = jnp.exp(s - m_new)
        l_new = a * l_old + jnp.sum(p, axis=-1, keepdims=True)
        vc = jnp.concatenate([vbuf[slot, g].astype(BF16) for g in range(chunk)], axis=-1)
        acc = a * acc + _dot_nt(p.astype(BF16), vc)
        return m_new, l_new, acc, run

    def pair(j, carry):
        carry = step(2 * j, 0, carry)
        return step(2 * j + 1, 1, carry)

    init = (jnp.full((rows, 1), NEG_INF, F32), jnp.zeros((rows, 1), F32),
            jnp.zeros((rows, C_KV_W), F32), jnp.zeros((C_HEADS, PAGE_SIZE), F32))
    m_old, l_old, acc, _ = lax.fori_loop(0, n_chunks // 2, pair, init)

    s = _dot_nt(qbd, kn_ref[0]) + cn - cnk_ref[0]
    ti = lax.broadcasted_iota(jnp.int32, (rows, LANES), 0) // C_HEADS
    sj = lax.broadcasted_iota(jnp.int32, (rows, LANES), 1)
    s = jnp.where(sj <= ti, s, NEG_INF)
    m_new = jnp.maximum(m_old, jnp.max(s, axis=-1, keepdims=True))
    a = jnp.exp(m_old - m_new)
    p = jnp.exp(s - m_new)
    l_new = a * l_old + jnp.sum(p, axis=-1, keepdims=True)
    acc = (a * acc + _dot(p.astype(BF16), vn_ref[0])) / l_new
    kv_of_row = (lax.broadcasted_iota(jnp.int32, (rows, C_HEAD_DIM), 0) % C_HEADS) // C_GROUP
    out = jnp.zeros((rows, C_HEAD_DIM), F32)
    for kvh in range(C_KV_HEADS):
        out = jnp.where(kv_of_row == kvh, acc[:, kvh * C_HEAD_DIM:(kvh + 1) * C_HEAD_DIM], out)
    o_ref[0] = out


def _fox_decode(page_table, qbd, cn_col, cn_key, k_new, v_new, kpool, vpool, lpool):
    db, n_pages = page_table.shape
    t = qbd.shape[1] // C_HEADS
    rows = t * C_HEADS
    chunk = DECODE_PAGES_PER_CHUNK
    assert n_pages % (2 * chunk) == 0
    j = np.arange(PAGE_SIZE)
    tri = np.concatenate([(j[:, None] > j[None, :]).astype(np.float32),
                          np.ones((PAGE_SIZE, PAGE_SIZE), np.float32)], axis=1)
    tri = jnp.asarray(tri, BF16)
    per_b = lambda shape: pl.BlockSpec((1,) + shape, lambda b, pt: (b, 0, 0))
    hbm = pl.BlockSpec(memory_space=pl.ANY)
    grid_spec = pltpu.PrefetchScalarGridSpec(
        num_scalar_prefetch=1,
        grid=(db,),
        in_specs=[per_b((rows, C_KV_W)), per_b((rows, LANES)), per_b((rows, LANES)),
                  per_b((LANES, C_KV_W)), per_b((LANES, C_KV_W)),
                  pl.BlockSpec((PAGE_SIZE, 2 * PAGE_SIZE), lambda b, pt: (0, 0)),
                  hbm, hbm, hbm],
        out_specs=per_b((rows, C_HEAD_DIM)),
        scratch_shapes=[pltpu.VMEM((2, chunk, C_KV_W, PAGE_SIZE), F32),
                        pltpu.VMEM((2, chunk, C_KV_W, PAGE_SIZE), F32),
                        pltpu.VMEM((2, chunk, C_HEADS, PAGE_SIZE), F32),
                        pltpu.SemaphoreType.DMA((2,))])
    return pl.pallas_call(
        functools.partial(_fox_decode_kernel, n_pages=n_pages, chunk=chunk, t=t),
        grid_spec=grid_spec,
        out_shape=jax.ShapeDtypeStruct((db, rows, C_HEAD_DIM), F32),
        compiler_params=_params("arbitrary"),
        name="fox_decode",
    )(page_table, qbd, cn_col, cn_key, k_new, v_new, tri, kpool, vpool, lpool)


_ROUT_GROUP_LANE = 8


def _router_kernel(x_ref, whi_ref, wlo_ref, b_ref, o_ref):
    x = x_ref[...]
    xh = x.astype(BF16)
    xl = (x - xh.astype(F32)).astype(BF16)
    logit = _dot(xh, whi_ref[...]) + _dot(xl, whi_ref[...]) + _dot(xh, wlo_ref[...]) + b_ref[...]
    lane_i = lax.broadcasted_iota(jnp.int32, logit.shape, 1)
    lane = lane_i.astype(F32)
    e_lane = lane_i - N_GROUPS
    lane_group = lax.shift_right_arithmetic(e_lane, 2).astype(F32)
    lane_slot = jnp.bitwise_and(e_lane, EXPERTS_PER_GROUP - 1).astype(F32)
    is_e = (e_lane >= 0) & (e_lane < N_EXPERTS)

    def lane_max(mask):
        return jnp.max(jnp.where(mask, logit, -jnp.inf), axis=-1, keepdims=True)

    def first_lane(mask):
        return jnp.min(jnp.where(mask, lane, float(LANES)), axis=-1, keepdims=True)

    is_g = lane_i < N_GROUPS
    g_max = lane_max(is_g)
    g_idx = first_lane(is_g & (logit == g_max))
    g_sel = 1.0 / jnp.sum(jnp.where(is_g, jnp.exp(logit - g_max), 0.0), axis=-1, keepdims=True)
    in_g = is_e & (lane_group == g_idx)
    v1 = lane_max(in_g)
    i1 = first_lane(in_g & (logit == v1))
    rest = in_g & (lane != i1)
    v2 = lane_max(rest)
    i2 = first_lane(rest & (logit == v2))
    e2 = jnp.exp(v2 - v1)
    w1 = g_sel / (1.0 + e2)
    w2 = g_sel * e2 / (1.0 + e2)
    j1 = jnp.sum(jnp.where(lane == i1, lane_slot, 0.0), axis=-1, keepdims=True)
    j2 = jnp.sum(jnp.where(lane == i2, lane_slot, 0.0), axis=-1, keepdims=True)
    out = jnp.where(lane == j1, w1, 0.0) + jnp.where(lane == j2, w2, 0.0)
    o_ref[...] = jnp.where(lane_i == _ROUT_GROUP_LANE, g_idx, out)


def _router(x, w_rg, b_rg, w_re, b_re):
    n, d = x.shape
    bm = TOKEN_BLOCK
    w = jnp.concatenate([w_rg, w_re, jnp.zeros((d, LANES - N_GROUPS - N_EXPERTS), F32)], axis=1)
    whi = w.astype(BF16)
    wlo = (w - whi.astype(F32)).astype(BF16)
    bias = jnp.concatenate([b_rg, b_re, jnp.zeros((LANES - N_GROUPS - N_EXPERTS,), F32)]).reshape(1, LANES)
    return pl.pallas_call(
        _router_kernel,
        grid=(n // bm,),
        in_specs=[pl.BlockSpec((bm, d), lambda i: (i, 0)),
                  pl.BlockSpec((d, LANES), lambda i: (0, 0)),
                  pl.BlockSpec((d, LANES), lambda i: (0, 0)),
                  pl.BlockSpec((1, LANES), lambda i: (0, 0))],
        out_specs=pl.BlockSpec((bm, LANES), lambda i: (i, 0)),
        out_shape=jax.ShapeDtypeStruct((n, LANES), F32),
        compiler_params=_params("arbitrary"),
        name="router",
    )(x, whi, wlo, bias)


def _moe_kernel(tg_ref, nu_ref, src_ref, x_hbm, comb_ref, wg_ref, wu_ref, wd_ref, g_ref, b_ref,
                out_hbm, xbuf, obuf, gsem, ssem, *, tm, n_tok):
    i = pl.program_id(0)
    n_used = nu_ref[0]

    def gather(tile, slot, wait):
        for r in range(tm):
            tok = jnp.maximum(src_ref[tile * tm + r], 0)
            cp = pltpu.make_async_copy(x_hbm.at[pl.ds(tok, 1)], xbuf.at[slot, pl.ds(r, 1)], gsem.at[slot])
            cp.wait() if wait else cp.start()

    def scatter(tile, wait):
        for r in range(tm):
            tok = src_ref[tile * tm + r]
            row = jnp.where(tok >= 0, tok, n_tok + r)
            cp = pltpu.make_async_copy(obuf.at[pl.ds(r, 1)], out_hbm.at[pl.ds(row, 1)], ssem.at[0])
            cp.wait() if wait else cp.start()

    @pl.when(i == 0)
    def _():
        gather(0, 0, False)
        obuf[...] = jnp.zeros_like(obuf)
        spare = pltpu.make_async_copy(obuf, out_hbm.at[pl.ds(n_tok, tm)], ssem.at[0])
        spare.start()
        spare.wait()

    @pl.when(i < n_used)
    def _():
        slot = i % 2

        @pl.when(i + 1 < n_used)
        def _():
            gather(i + 1, 1 - slot, False)

        gather(i, slot, True)
        x = xbuf[slot]
        xb = x.astype(BF16)
        comb = comb_ref[...]
        y = jnp.zeros(x.shape, F32)
        for e in range(EXPERTS_PER_GROUP):
            gate = _dot(xb, wg_ref[0, e])
            up = _dot(xb, wu_ref[0, e])
            h = gate * _sigmoid(gate) * up * comb[:, e:e + 1]
            y = y + _dot(h.astype(BF16), wd_ref[0, e])
        res = _layer_norm(DEEPNORM_ALPHA * x + y, g_ref[...], b_ref[...])

        @pl.when(i > 0)
        def _():
            scatter(i - 1, True)

        obuf[...] = res
        scatter(i, False)

        @pl.when(i == n_used - 1)
        def _():
            scatter(i, True)


def _moe(x, rout, w_gate, w_up, w_down, g, b):
    n, d = x.shape
    tm = MOE_TILE
    n_tiles = n // tm + N_GROUPS
    gidx = rout[:, _ROUT_GROUP_LANE].astype(jnp.int32)
    onehot = (gidx[:, None] == jnp.arange(N_GROUPS, dtype=jnp.int32)[None, :]).astype(jnp.int32)
    counts = jnp.sum(onehot, axis=0)
    rank = jnp.sum((jnp.cumsum(onehot, axis=0) - onehot) * onehot, axis=1)
    tile_off = jnp.concatenate([jnp.zeros((1,), jnp.int32), jnp.cumsum((counts + tm - 1) // tm)]).astype(jnp.int32)
    pos = tile_off[gidx] * tm + rank
    src = jnp.full((n_tiles * tm,), -1, jnp.int32).at[pos].set(jnp.arange(n, dtype=jnp.int32))
    n_used = tile_off[N_GROUPS:]
    tiles = jnp.minimum(jnp.arange(n_tiles, dtype=jnp.int32), n_used[0] - 1)
    tile_group = jnp.sum((tiles[:, None] >= tile_off[None, 1:N_GROUPS]).astype(jnp.int32), axis=1)
    comb = jnp.where(src[:, None] >= 0, rout[jnp.maximum(src, 0), :EXPERTS_PER_GROUP], 0.0)

    shape4 = (N_GROUPS, EXPERTS_PER_GROUP)
    wg = w_gate.astype(BF16).reshape(shape4 + w_gate.shape[1:])
    wu = w_up.astype(BF16).reshape(shape4 + w_up.shape[1:])
    wd = w_down.astype(BF16).reshape(shape4 + w_down.shape[1:])
    wmap = lambda i, tg, nu, src: (tg[i], 0, 0, 0)
    full = lambda i, tg, nu, src: (0, 0)
    hbm = pl.BlockSpec(memory_space=pl.ANY)
    grid_spec = pltpu.PrefetchScalarGridSpec(
        num_scalar_prefetch=3,
        grid=(n_tiles,),
        in_specs=[hbm,
                  pl.BlockSpec((tm, EXPERTS_PER_GROUP), lambda i, tg, nu, src: (i, 0)),
                  pl.BlockSpec((1, EXPERTS_PER_GROUP, d, D_EXPERT), wmap),
                  pl.BlockSpec((1, EXPERTS_PER_GROUP, d, D_EXPERT), wmap),
                  pl.BlockSpec((1, EXPERTS_PER_GROUP, D_EXPERT, d), wmap),
                  pl.BlockSpec((1, d), full),
                  pl.BlockSpec((1, d), full)],
        out_specs=hbm,
        scratch_shapes=[pltpu.VMEM((2, tm, d), F32),
                        pltpu.VMEM((tm, d), F32),
                        pltpu.SemaphoreType.DMA((2,)),
                        pltpu.SemaphoreType.DMA((1,))])
    return pl.pallas_call(
        functools.partial(_moe_kernel, tm=tm, n_tok=n),
        grid_spec=grid_spec,
        out_shape=jax.ShapeDtypeStruct((n + tm, d), F32),
        compiler_params=_params("arbitrary"),
        name="moe",
    )(tile_group, n_used, src, x, comb, wg, wu, wd, g.reshape(1, d), b.reshape(1, d))


def _mem_and_moe(xp, xs, mem2, cache_mem_k, cache_mem_v, layer, bsz, db, t_s, p):
    d = D_MODEL
    n_mem = mem2.shape[0] // bsz
    wq, wo = p['w_xq'][layer].astype(BF16), p['w_xo'][layer].astype(BF16)
    g1, b1 = p['ln_g'][layer, 1], p['ln_b'][layer, 1]
    wkv = jnp.concatenate([p['w_xk'][layer], p['w_xv'][layer]], axis=1).astype(BF16)
    mkv = _matmul(mem2, wkv)
    mk, mv = mkv[:, :d], mkv[:, d:]
    xp = _xattn_prompt(xp, mk, mv, wq, wo, g1, b1, bsz)
    qs = _matmul(xs, wq).reshape(db, t_s, d)
    os_ = _xattn_sample(qs, cache_mem_k, cache_mem_v, layer)
    xs = _proj_norm([os_.reshape(db * t_s, d)], [wo], xs, g1, b1)
    x = jnp.concatenate([xp, xs], axis=0)
    rout = _router(x, p['w_rg'][layer], p['b_rg'][layer], p['w_re'][layer], p['b_re'][layer])
    x = _moe(x, rout, p['w_gate'][layer], p['w_up'][layer], p['w_down'][layer],
             p['ln_g'][layer, 2], p['ln_b'][layer, 2])
    n_p, n_s = xp.shape[0], xs.shape[0]
    return (x[:n_p], x[n_p:n_p + n_s],
            mk.reshape(bsz, n_mem, MEM_HEADS, MEM_HEAD_DIM), mv.reshape(bsz, n_mem, MEM_HEADS, MEM_HEAD_DIM))


def kernel(x_prompt, x_sample, mem_prompt, cache_swa_k, cache_swa_v, state_ret, cache_fox_k, cache_fox_v, cache_fox_logf, cache_mem_k, cache_mem_v, page_table, w_in_ab, sink_ab, gn_gain_ab, w_out_ab, w_in_fox, b_forget, w_out_fox, w_xq, w_xk, w_xv, w_xo, ln_g, ln_b, w_rg, b_rg, w_re, b_re, w_gate, w_up, w_down):
    p = dict(w_xq=w_xq, w_xk=w_xk, w_xv=w_xv, w_xo=w_xo, ln_g=ln_g, ln_b=ln_b, w_rg=w_rg, b_rg=b_rg,
             w_re=w_re, b_re=b_re, w_gate=w_gate, w_up=w_up, w_down=w_down)
    bsz, seq, d = x_prompt.shape
    db, t_s, _ = x_sample.shape
    past_len = page_table.shape[1] * PAGE_SIZE
    xp = x_prompt.reshape(bsz * seq, d)
    xs = x_sample.reshape(db * t_s, d)
    mem2 = mem_prompt.reshape(-1, d)
    out = {}

    w_in = _permute_w_in_ab(w_in_ab[0]).astype(BF16)
    w_out = w_out_ab[0].astype(BF16)
    g0, b0 = ln_g[0, 0], ln_b[0, 0]
    zp = _matmul(xp, w_in)
    a_o = _swa_prompt(zp, sink_ab[0], seq)
    cos_p, sin_p = _rope_tables(jnp.arange(seq, dtype=jnp.int32))
    b_y, ret_p = _ret_prompt(zp.reshape(bsz, seq, _AB_W), cos_p, sin_p, gn_gain_ab[0])
    zp3 = zp.reshape(bsz, seq, _AB_W)
    out['swa_kp'] = zp3[:, seq - WINDOW:, _AB_AK:_AB_AK + A_KV_W].reshape(1, bsz, WINDOW, A_KV_HEADS, A_HEAD_DIM)
    out['swa_vp'] = zp3[:, seq - WINDOW:, _AB_AV:_AB_AV + A_KV_W].reshape(1, bsz, WINDOW, A_KV_HEADS, A_HEAD_DIM)
    xp = _proj_norm([a_o, b_y.reshape(bsz * seq, B_W)], [w_out[:A_Q_W], w_out[A_Q_W:]], xp, g0, b0)

    zs = _matmul(xs, w_in).reshape(db, t_s, _AB_W)
    cos_s, sin_s = _rope_tables(past_len + jnp.arange(t_s, dtype=jnp.int32))
    h_s, kb, vb, ret_s = _ab_sample(zs, cache_swa_k[0].reshape(db, WINDOW, A_KV_W),
                                    cache_swa_v[0].reshape(db, WINDOW, A_KV_W), state_ret[0],
                                    sink_ab[0], cos_s, sin_s, gn_gain_ab[0])
    xs = _proj_norm([h_s.reshape(db * t_s, A_Q_W + B_W)], [w_out], xs, g0, b0)
    xp, xs, mk0, mv0 = _mem_and_moe(xp, xs, mem2, cache_mem_k, cache_mem_v, 0, bsz, db, t_s, p)

    w_in = jnp.concatenate([w_in_fox[0], jnp.zeros((d, LANES - C_HEADS), F32)], axis=1).astype(BF16)
    w_out = w_out_fox[0].astype(BF16)
    g0, b0 = ln_g[1, 0], ln_b[1, 0]
    kc0, vc0 = C_Q_W, C_Q_W + C_KV_W
    zf = _matmul(xp, w_in)
    lf_p, qa, ka, va = _fox_features(zf, b_forget[0], seq)
    o_p = _fox_prompt(qa, ka, va, bsz, seq)
    xp = _proj_norm([o_p], [w_out], xp, g0, b0)
    fox_kp = zf[:, kc0:kc0 + C_KV_W].reshape(1, bsz, seq, C_KV_HEADS, C_HEAD_DIM)
    fox_vp = zf[:, vc0:vc0 + C_KV_W].reshape(1, bsz, seq, C_KV_HEADS, C_HEAD_DIM)

    zfs = _matmul(xs, w_in)
    lf_s, cn = _logf_cumsum(zfs, b_forget[0], t_s)
    rows = t_s * C_HEADS
    q = zfs[:, :C_Q_W].reshape(db, t_s, C_HEADS, 1, C_HEAD_DIM) * (C_HEAD_DIM ** -0.5)
    own_kv = (jnp.arange(C_HEADS)[:, None] // C_GROUP == jnp.arange(C_KV_HEADS)[None, :])
    qbd = jnp.where(own_kv[None, None, :, :, None], q, 0.0).reshape(db, rows, C_KV_W).astype(BF16)
    cn3 = cn.reshape(db, t_s, C_HEADS)
    cn_col = jnp.broadcast_to(cn3.reshape(db, rows, 1), (db, rows, LANES))
    cn_key = jnp.broadcast_to(cn3.transpose(0, 2, 1)[:, None], (db, t_s, C_HEADS, t_s)).reshape(db, rows, t_s)
    cn_key = jnp.concatenate([cn_key, jnp.zeros((db, rows, LANES - t_s), F32)], axis=-1)
    k_new = zfs[:, kc0:kc0 + C_KV_W].reshape(db, t_s, C_KV_W)
    v_new = zfs[:, vc0:vc0 + C_KV_W].reshape(db, t_s, C_KV_W)
    zpad = jnp.zeros((db, LANES - t_s, C_KV_W), F32)
    o_s = _fox_decode(page_table, qbd, cn_col, cn_key,
                      jnp.concatenate([k_new, zpad], axis=1).astype(BF16),
                      jnp.concatenate([v_new, zpad], axis=1).astype(BF16),
                      cache_fox_k[0].transpose(0, 2, 3, 1).reshape(-1, C_KV_W, PAGE_SIZE),
                      cache_fox_v[0].transpose(0, 2, 3, 1).reshape(-1, C_KV_W, PAGE_SIZE),
                      cache_fox_logf[0].transpose(0, 2, 1))
    xs = _proj_norm([o_s.reshape(db * t_s, C_Q_W)], [w_out], xs, g0, b0)
    xp, xs, mk1, mv1 = _mem_and_moe(xp, xs, mem2, cache_mem_k, cache_mem_v, 1, bsz, db, t_s, p)

    kv5 = (1, db, t_s, C_KV_HEADS, C_HEAD_DIM)
    return (xp.reshape(bsz, seq, d), xs.reshape(db, t_s, d),
            out['swa_kp'], out['swa_vp'],
            kb.reshape(1, db, WINDOW, A_KV_HEADS, A_HEAD_DIM), vb.reshape(1, db, WINDOW, A_KV_HEADS, A_HEAD_DIM),
            ret_p[None], ret_s[None],
            fox_kp, fox_vp, lf_p.reshape(1, bsz, seq, C_HEADS),
            k_new.reshape(kv5), v_new.reshape(kv5), lf_s.reshape(1, db, t_s, C_HEADS),
            jnp.stack([mk0, mk1]), jnp.stack([mv0, mv1]))
```
